```python
import math
import jax, jax.numpy as jnp
from jax import lax
import numpy as np

D_MODEL = 2048
BATCH = 2
SEQ = 8192
DEPTH = 1
DEC_BATCH = 32
DEC_SEQ = 8
PAST_LEN = 16384
PAGE_SIZE = 128

HEAD_DIM = 64
D_MIX = D_MODEL
RWKV_WIDTH = D_MIX // 2
ATT_WIDTH = D_MIX - RWKV_WIDTH
N_RWKV_HEADS = RWKV_WIDTH // HEAD_DIM
N_ATT_HEADS = ATT_WIDTH // HEAD_DIM
DECAY_LORA = 64
ICLR_LORA = 64
GATE_LORA = 160
GN_EPS = 64e-5
SHIFT_SPLITS = (RWKV_WIDTH, 2 * RWKV_WIDTH, 3 * RWKV_WIDTH, 3 * RWKV_WIDTH + DECAY_LORA,
                3 * RWKV_WIDTH + DECAY_LORA + ICLR_LORA)
N_SHIFT = 3 * RWKV_WIDTH + DECAY_LORA + ICLR_LORA + GATE_LORA
N_IN = N_SHIFT + 3 * ATT_WIDTH
MOBA_BLOCK = 256
MOBA_TOPK = 3
Q_BLOCK = 128
N_BUCKETS = 32
MAX_EXACT = N_BUCKETS // 2
MAX_DISTANCE = 128
N_EXPERTS = 32
TOP_K = 4
D_FF = D_MODEL
SWIGLU_LIMIT = 7.0
SWIGLU_ALPHA = 1.702
D_PLE = 256
RMS_EPS = 1e-6
NEG = -1e30

kernel_name = 'hymba_rwkv7_moba_moe_step'


def _rmsnorm(x, g):
    xf = x.astype(jnp.float32)
    y = xf * lax.rsqrt(jnp.mean(xf * xf, axis=-1, keepdims=True) + RMS_EPS)
    return (y * g.astype(jnp.float32)).astype(x.dtype)


def _t5_bucket(dist):
    n = jnp.maximum(dist, 0)
    nf = jnp.maximum(n, 1).astype(jnp.float32)
    large = MAX_EXACT + (jnp.log(nf / MAX_EXACT) / math.log(MAX_DISTANCE / MAX_EXACT)
                         * (N_BUCKETS - MAX_EXACT)).astype(jnp.int32)
    large = jnp.minimum(large, N_BUCKETS - 1)
    return jnp.where(n < MAX_EXACT, n, large)


def _rwkv7(u, shift_prev, wkv0, mu, w0, w_up, a0, a_up, g_up, k_k, k_a, r_k, lnx_w, lnx_b):
    B, T, _ = u.shape
    f32 = jnp.float32
    H, N = N_RWKV_HEADS, HEAD_DIM
    shifted = jnp.concatenate([shift_prev[:, None, :].astype(u.dtype), u[:, :-1]], axis=1)
    xs = u + (shifted - u) * mu
    r, k, v, dw, da, dg = jnp.split(xs, SHIFT_SPLITS, axis=-1)
    w_log = -jax.nn.softplus(-(w0 + jnp.tanh(dw) @ w_up).astype(f32)) - 0.5
    decay = jnp.exp(-jnp.exp(w_log))
    a = jax.nn.sigmoid((a0 + da @ a_up).astype(f32))
    g = jax.nn.sigmoid(dg) @ g_up
    heads = lambda t: t.astype(f32).reshape(B, T, H, N)
    r, k, v, decay, a = heads(r), heads(k), heads(v), heads(decay), heads(a)
    kk = k * k_k.astype(f32).reshape(H, N)
    kk = kk / jnp.maximum(jnp.sqrt(jnp.sum(kk * kk, axis=-1, keepdims=True)), 1e-12)
    k = k * (1.0 + (a - 1.0) * k_a.astype(f32).reshape(H, N))

    def step(S, inp):
        r_t, w_t, k_t, v_t, kk_t, a_t = inp
        sk = jnp.einsum('bhij,bhj->bhi', S, kk_t)
        S = (S * w_t[:, :, None, :] - sk[..., None] * (kk_t * a_t)[:, :, None, :]
             + v_t[..., None] * k_t[:, :, None, :])
        return S, jnp.einsum('bhij,bhj->bhi', S, r_t)

    tm = lambda t: jnp.moveaxis(t, 1, 0)
    S_fin, o = lax.scan(step, wkv0.astype(f32), (tm(r), tm(decay), tm(k), tm(v), tm(kk), tm(a)))
    o = jnp.moveaxis(o, 0, 1)
    mean = jnp.mean(o, axis=-1, keepdims=True)
    var = jnp.mean(jnp.square(o - mean), axis=-1, keepdims=True)
    o = ((o - mean) * lax.rsqrt(var + GN_EPS)).reshape(B, T, RWKV_WIDTH) * lnx_w + lnx_b
    bonus = (jnp.sum(r * k * r_k.astype(f32), axis=-1, keepdims=True) * v).reshape(B, T, RWKV_WIDTH)
    out = ((o + bonus) * g).astype(u.dtype)
    return out, u[:, -1, :], S_fin


def _split_qkv(u_att):
    B, T, _ = u_att.shape
    a = u_att.reshape(B, T, 3, N_ATT_HEADS, HEAD_DIM)
    return a[:, :, 0], a[:, :, 1], a[:, :, 2]


def _moba_attend(q, q_pos, k_own, v_own, kpos_own, sel, rel_bias):
    f32 = jnp.float32
    H = q.shape[2]
    scale = HEAD_DIM ** -0.5
    d_own = q_pos[:, None] - kpos_own[None, :]
    s_own = jnp.einsum('bqhd,bkhd->bqhk', q, k_own, preferred_element_type=f32) * scale
    s_own = s_own + jnp.transpose(rel_bias[_t5_bucket(d_own)], (0, 2, 1))[None].astype(f32)
    s_own = jnp.where((d_own >= 0)[None, :, None, :], s_own, NEG)
    if sel is None:
        p = jax.nn.softmax(s_own, axis=-1).astype(v_own.dtype)
        return jnp.einsum('bqhk,bkhd->bqhd', p, v_own)
    k_sel, v_sel, kpos_sel, valid = sel
    B, Q, _, n, Kb, _ = k_sel.shape
    hi = jnp.arange(H)[:, None, None]
    d_sel = q_pos[None, :, None, None, None] - kpos_sel
    s_sel = jnp.einsum('bqhd,bqhnkd->bqhnk', q, k_sel, preferred_element_type=f32) * scale
    s_sel = s_sel + rel_bias[_t5_bucket(d_sel), hi].astype(f32)
    s_sel = jnp.where(valid[..., None], s_sel, NEG)
    logits = jnp.concatenate([s_sel.reshape(B, Q, H, n * Kb), s_own], axis=-1)
    p = jax.nn.softmax(logits, axis=-1).astype(v_own.dtype)
    p_sel = p[..., :n * Kb].reshape(B, Q, H, n, Kb)
    p_own = p[..., n * Kb:]
    return (jnp.einsum('bqhnk,bqhnkd->bqhd', p_sel, v_sel)
            + jnp.einsum('bqhk,bkhd->bqhd', p_own, v_own))


def _moba_prompt(q, k, v, rel_bias):
    B, T, H, Dh = q.shape
    nb = -(-T // MOBA_BLOCK)
    padn = nb * MOBA_BLOCK - T
    pad = ((0, 0), (0, padn), (0, 0), (0, 0))
    k_blocks = jnp.pad(k, pad).reshape(B, nb, MOBA_BLOCK, H, Dh)
    v_blocks = jnp.pad(v, pad).reshape(B, nb, MOBA_BLOCK, H, Dh)
    nbf = T // MOBA_BLOCK
    n_sel = min(MOBA_TOPK, nbf)
    k_mean = jnp.mean(k_blocks[:, :nbf], axis=2, dtype=jnp.float32)
    nq = T // Q_BLOCK
    q_blocks = jnp.moveaxis(q.reshape(B, nq, Q_BLOCK, H, Dh), 1, 0)
    bi = jnp.arange(B)[:, None, None, None]
    hi = jnp.arange(H)[:, None]

    def one_block(args):
        qi, qb = args
        q_pos = qi * Q_BLOCK + jnp.arange(Q_BLOCK)
        own = (qi * Q_BLOCK) // MOBA_BLOCK
        k_own = lax.dynamic_index_in_dim(k_blocks, own, axis=1, keepdims=False)
        v_own = lax.dynamic_index_in_dim(v_blocks, own, axis=1, keepdims=False)
        kpos_own = own * MOBA_BLOCK + jnp.arange(MOBA_BLOCK)
        sel = None
        if n_sel > 0:
            gate = jnp.einsum('bqhd,bnhd->bqhn', qb.astype(jnp.float32), k_mean)
            gate = jnp.where(jnp.arange(nbf) < own, gate, NEG)
            _, idx = lax.top_k(gate, n_sel)
            k_sel = k_blocks[bi, idx, :, hi]
            v_sel = v_blocks[bi, idx, :, hi]
            kpos_sel = idx[..., None] * MOBA_BLOCK + jnp.arange(MOBA_BLOCK)
            sel = (k_sel, v_sel, kpos_sel, idx < own)
        return _moba_attend(qb, q_pos, k_own, v_own, kpos_own, sel, rel_bias)

    out = lax.map(one_block, (jnp.arange(nq), q_blocks))
    return jnp.moveaxis(out, 0, 1).reshape(B, T, H, Dh)


def _moba_sample(q, k_new, v_new, pool_k, pool_v, page_table, rel_bias):
    DB, T, H, Dh = q.shape
    ppb = MOBA_BLOCK // PAGE_SIZE
    own = PAST_LEN // MOBA_BLOCK
    tail_start = own * MOBA_BLOCK
    tail = PAST_LEN - tail_start
    tail_pt = page_table[:, tail_start // PAGE_SIZE:]
    k_own = jnp.concatenate([pool_k[tail_pt].reshape(DB, tail, H, Dh), k_new], axis=1)
    v_own = jnp.concatenate([pool_v[tail_pt].reshape(DB, tail, H, Dh), v_new], axis=1)
    kpos_own = tail_start + jnp.arange(tail + T)
    q_pos = PAST_LEN + jnp.arange(T)
    nbf = own
    n_sel = min(MOBA_TOPK, nbf)
    sel = None
    if n_sel > 0:
        pt_full = page_table[:, :nbf * ppb]
        k_mean = jnp.mean(pool_k[pt_full].reshape(DB, nbf, MOBA_BLOCK, H, Dh), axis=2,
                          dtype=jnp.float32)
        gate = jnp.einsum('bqhd,bnhd->bqhn', q.astype(jnp.float32), k_mean)
        _, idx = lax.top_k(gate, n_sel)
        bi = jnp.arange(DB)[:, None, None, None, None]
        phys = page_table[bi, idx[..., None] * ppb + jnp.arange(ppb)]
        hi = jnp.arange(H)[:, None, None]
        k_sel = pool_k[phys, :, hi].reshape(DB, T, H, n_sel, MOBA_BLOCK, Dh)
        v_sel = pool_v[phys, :, hi].reshape(DB, T, H, n_sel, MOBA_BLOCK, Dh)
        kpos_sel = idx[..., None] * MOBA_BLOCK + jnp.arange(MOBA_BLOCK)
        sel = (k_sel, v_sel, kpos_sel, idx < own)
    return _moba_attend(q, q_pos, k_own, v_own, kpos_own, sel, rel_bias)


def _moe(x, router_w, router_b, w_gu, b_gu, w_dn, b_dn):
    logits = (x @ router_w).astype(jnp.float32) + router_b.astype(jnp.float32)
    top_vals, top_idx = lax.top_k(logits, TOP_K)
    gates = jax.nn.softmax(top_vals, axis=-1)
    dense_gate = jnp.sum(jax.nn.one_hot(top_idx, N_EXPERTS, dtype=jnp.float32) * gates[..., None],
                         axis=-2).astype(x.dtype)
    y = jnp.zeros_like(x)
    for e in range(N_EXPERTS):
        gu = x @ w_gu[e] + b_gu[e]
        glu = jnp.minimum(gu[:, :D_FF], SWIGLU_LIMIT)
        lin = jnp.clip(gu[:, D_FF:], -SWIGLU_LIMIT, SWIGLU_LIMIT)
        act = glu * jax.nn.sigmoid(SWIGLU_ALPHA * glu) * (lin + 1.0)
        y = y + dense_gate[:, e:e + 1] * (act @ w_dn[e] + b_dn[e])
    return y


def setup_inputs(seed: int = 0) -> dict:
    key = jax.random.key(seed)
    ks = iter(jax.random.split(key, 48))
    f32 = jnp.float32
    nrm = lambda shape, scale: jax.random.normal(next(ks), shape, f32) * scale
    uni = lambda shape, lo, hi: jax.random.uniform(next(ks), shape, f32, lo, hi)
    n_pages = PAST_LEN // PAGE_SIZE
    n_used = DEC_BATCH * n_pages
    n_pool = n_used + max(1, n_used // 4)
    page_table = jax.random.permutation(next(ks), n_pool)[:n_used].reshape(DEC_BATCH, n_pages).astype(jnp.int32)
    L, D, E = DEPTH, D_MODEL, N_EXPERTS
    return {
        'x_prompt': nrm((BATCH, SEQ, D), 1.0),
        'x_sample': nrm((DEC_BATCH, DEC_SEQ, D), 1.0),
        'cache_k': nrm((L, n_pool, PAGE_SIZE, N_ATT_HEADS, HEAD_DIM), 1.0),
        'cache_v': nrm((L, n_pool, PAGE_SIZE, N_ATT_HEADS, HEAD_DIM), 1.0),
        'state_wkv': nrm((L, DEC_BATCH, N_RWKV_HEADS, HEAD_DIM, HEAD_DIM), 0.5),
        'state_shift': nrm((L, DEC_BATCH, N_SHIFT), 1.0),
        'page_table': page_table,
        'p_prompt': nrm((L, BATCH, SEQ, D_PLE), 1.0),
        'p_sample': nrm((L, DEC_BATCH, DEC_SEQ, D_PLE), 1.0),
        'norm1_g': 1.0 + nrm((L, D), 0.02),
        'w_in': nrm((L, D, N_IN), D ** -0.5),
        'mu_shift': uni((L, N_SHIFT), 0.0, 1.0),
        'w0': uni((L, RWKV_WIDTH), -6.0, 1.0),
        'w_up': nrm((L, DECAY_LORA, RWKV_WIDTH), 0.5 * DECAY_LORA ** -0.5),
        'a0': nrm((L, RWKV_WIDTH), 0.1),
        'a_up': nrm((L, ICLR_LORA, RWKV_WIDTH), 0.5 * ICLR_LORA ** -0.5),
        'g_up': nrm((L, GATE_LORA, RWKV_WIDTH), GATE_LORA ** -0.5),
        'k_k': 0.85 + nrm((L, RWKV_WIDTH), 0.02),
        'k_a': 1.0 + nrm((L, RWKV_WIDTH), 0.02),
        'r_k': nrm((L, N_RWKV_HEADS, HEAD_DIM), 0.1),
        'lnx_w': 1.0 + nrm((L, RWKV_WIDTH), 0.02),
        'lnx_b': nrm((L, RWKV_WIDTH), 0.02),
        'w_out': nrm((L, D_MIX, D), D_MIX ** -0.5),
        'rel_bias': nrm((N_BUCKETS, N_ATT_HEADS), 0.1),
        'norm2_g': 1.0 + nrm((L, D), 0.02),
        'router_w': nrm((L, D, E), D ** -0.5),
        'router_b': nrm((L, E), 0.01),
        'w_gu': nrm((L, E, D, 2 * D_FF), D ** -0.5),
        'b_gu': nrm((L, E, 2 * D_FF), 0.01),
        'w_dn': nrm((L, E, D_FF, D), D_FF ** -0.5),
        'b_dn': nrm((L, E, D), 0.01),
        'ple_norm_g': 1.0 + nrm((L, D), 0.02),
        'ple_gate_w': nrm((L, D, D), D ** -0.5),
        'ple_proj': nrm((L, D_PLE, D), D_PLE ** -0.5),
        'final_norm_g': 1.0 + nrm((D,), 0.02),
    }


def reference(x_prompt, x_sample, cache_k, cache_v, state_wkv, state_shift, page_table, p_prompt, p_sample,
              norm1_g, w_in, mu_shift, w0, w_up, a0, a_up, g_up, k_k, k_a, r_k, lnx_w, lnx_b, w_out,
              rel_bias, norm2_g, router_w, router_b, w_gu, b_gu, w_dn, b_dn, ple_norm_g, ple_gate_w,
              ple_proj, final_norm_g):
    B, S, _ = x_prompt.shape
    DB, T, _ = x_sample.shape
    n_prompt = B * S
    hp, hs = x_prompt, x_sample
    kp_l, vp_l, wp_l, sp_l, ks_l, vs_l, ws_l, ss_l = [], [], [], [], [], [], [], []
    for i in range(DEPTH):
        rw = (mu_shift[i], w0[i], w_up[i], a0[i], a_up[i], g_up[i], k_k[i], k_a[i], r_k[i],
              lnx_w[i], lnx_b[i])
        up = _rmsnorm(hp, norm1_g[i]) @ w_in[i]
        us = _rmsnorm(hs, norm1_g[i]) @ w_in[i]
        ro_p, sh_p, wkv_p = _rwkv7(up[..., :N_SHIFT], jnp.zeros((B, N_SHIFT), up.dtype),
                                   jnp.zeros((B, N_RWKV_HEADS, HEAD_DIM, HEAD_DIM), jnp.float32), *rw)
        ro_s, sh_s, wkv_s = _rwkv7(us[..., :N_SHIFT], state_shift[i], state_wkv[i], *rw)
        qp, kp, vp = _split_qkv(up[..., N_SHIFT:])
        qs, ks_, vs_ = _split_qkv(us[..., N_SHIFT:])
        ao_p = _moba_prompt(qp, kp, vp, rel_bias)
        ao_s = _moba_sample(qs, ks_, vs_, cache_k[i], cache_v[i], page_table, rel_bias)
        hp = hp + jnp.concatenate([ro_p, ao_p.reshape(B, S, ATT_WIDTH)], axis=-1) @ w_out[i]
        hs = hs + jnp.concatenate([ro_s, ao_s.reshape(DB, T, ATT_WIDTH)], axis=-1) @ w_out[i]
        h = jnp.concatenate([hp.reshape(n_prompt, D_MODEL), hs.reshape(DB * T, D_MODEL)], axis=0)
        pe = jnp.concatenate([p_prompt[i].reshape(n_prompt, D_PLE), p_sample[i].reshape(DB * T, D_PLE)], axis=0)
        h = h + _moe(_rmsnorm(h, norm2_g[i]), router_w[i], router_b[i], w_gu[i], b_gu[i], w_dn[i], b_dn[i])
        h = h + jax.nn.sigmoid(_rmsnorm(h, ple_norm_g[i]) @ ple_gate_w[i]) * (pe @ ple_proj[i])
        hp = h[:n_prompt].reshape(B, S, D_MODEL)
        hs = h[n_prompt:].reshape(DB, T, D_MODEL)
        kp_l.append(kp)
        vp_l.append(vp)
        wp_l.append(wkv_p.astype(x_prompt.dtype))
        sp_l.append(sh_p)
        ks_l.append(ks_)
        vs_l.append(vs_)
        ws_l.append(wkv_s.astype(state_wkv.dtype))
        ss_l.append(sh_s)
    y_prompt = _rmsnorm(hp, final_norm_g)
    y_sample = _rmsnorm(hs, final_norm_g)
    k_prompt = jnp.stack(kp_l)
    v_prompt = jnp.stack(vp_l)
    wkv_prompt = jnp.stack(wp_l)
    shift_prompt = jnp.stack(sp_l)
    k_sample = jnp.stack(ks_l)
    v_sample = jnp.stack(vs_l)
    wkv_sample = jnp.stack(ws_l)
    shift_sample = jnp.stack(ss_l)
    return (y_prompt, y_sample, k_prompt, v_prompt, wkv_prompt, shift_prompt, k_sample, v_sample, wkv_sample, shift_sample)
```

```python
import functools
import math

import jax
import jax.numpy as jnp
from jax import lax
from jax.experimental import pallas as pl
from jax.experimental.pallas import tpu as pltpu

f32 = jnp.float32
bf16 = jnp.bfloat16
i32 = jnp.int32
HI = lax.Precision.HIGHEST

D_MODEL = 2048
BATCH = 2
SEQ = 8192
DEC_BATCH = 32
DEC_SEQ = 8
PAST_LEN = 16384
PAGE_SIZE = 128
HEAD_DIM = 64
RWKV_WIDTH = 1024
ATT_WIDTH = 1024
N_HEADS = 16
DECAY_LORA = 64
ICLR_LORA = 64
GATE_LORA = 160
LORA_WIDTH = DECAY_LORA + ICLR_LORA + GATE_LORA
LORA_PAD = 384
GN_EPS = 64e-5
N_SHIFT = 3 * RWKV_WIDTH + LORA_WIDTH
MOBA_BLOCK = 256
MOBA_TOPK = 3
Q_BLOCK = 128
N_BUCKETS = 32
MAX_EXACT = 16
MAX_DISTANCE = 128
N_EXPERTS = 32
TOP_K = 4
D_FF = 2048
SWIGLU_LIMIT = 7.0
SWIGLU_ALPHA = 1.702
D_PLE = 256
RMS_EPS = 1e-6
NEG = -1e30

NP = BATCH * SEQ
NS = DEC_BATCH * DEC_SEQ
NTOK = NP + NS
N_PAGES = PAST_LEN // PAGE_SIZE
N_PAST_BLOCKS = PAST_LEN // MOBA_BLOCK
assert PAST_LEN % MOBA_BLOCK == 0

GROUP_W = 1024
SCAN_CHUNK = 64
MOE_TM = 512
MOE_TILES = -(-(NTOK * TOP_K) // MOE_TM) + N_EXPERTS
MOE_ROWS = MOE_TILES * MOE_TM
PAGES_PER_STEP = 8
VMEM_LIMIT = 56 << 20


def _cp(sem, vmem=VMEM_LIMIT):
    return pltpu.CompilerParams(dimension_semantics=sem, vmem_limit_bytes=vmem)


def _dot(a, b, precision=None):
    return jnp.dot(a, b, preferred_element_type=f32, precision=precision)


def _dot_nt(a, b, precision=None):
    return lax.dot_general(a, b, (((1,), (1,)), ((), ())), preferred_element_type=f32, precision=precision)


def _dot_tn(a, b, precision=None):
    return lax.dot_general(a, b, (((0,), (0,)), ((), ())), preferred_element_type=f32, precision=precision)


def _rms(x, g):
    ms = jnp.mean(x * x, axis=-1, keepdims=True)
    return x * lax.rsqrt(ms + RMS_EPS) * g


def _topk_mask(vals, n_valid, k):
    idx = lax.broadcasted_iota(i32, vals.shape, 1)
    idxf = idx.astype(f32)
    gm = jnp.where(idx < n_valid, vals, NEG)
    sel = jnp.zeros(vals.shape, f32)
    for _ in range(k):
        mx = jnp.max(gm, axis=1, keepdims=True)
        cand = jnp.where(gm == mx, jnp.where(gm > 0.5 * NEG, idxf, 1e9), 1e9)
        first = jnp.min(cand, axis=1, keepdims=True)
        pick = idxf == first
        sel = jnp.where(pick, 1.0, sel)
        gm = jnp.where(pick, NEG, gm)
    return sel


def _t5_bucket(d):
    n = jnp.maximum(d, 0)
    nf = jnp.maximum(n, 1).astype(f32)
    large = MAX_EXACT + (jnp.log(nf / MAX_EXACT) / math.log(MAX_DISTANCE / MAX_EXACT)
                         * (N_BUCKETS - MAX_EXACT)).astype(i32)
    large = jnp.minimum(large, N_BUCKETS - 1)
    return jnp.where(n < MAX_EXACT, n, large)


def _inproj_kernel(x_ref, g_ref, w_ref, o_ref, xn_ref):
    @pl.when(pl.program_id(1) == 0)
    def _():
        xn_ref[...] = _rms(x_ref[...], g_ref[...]).astype(bf16)

    o_ref[0] = _dot(xn_ref[...], w_ref[...])


def _inproj(x_all, g, w_big):
    n, d = x_all.shape
    tm = 640 if n % 640 == 0 else 256
    ng = w_big.shape[1] // GROUP_W
    return pl.pallas_call(
        _inproj_kernel,
        grid=(n // tm, ng),
        in_specs=[pl.BlockSpec((tm, d), lambda i, j: (i, 0)),
                  pl.BlockSpec((1, d), lambda i, j: (0, 0)),
                  pl.BlockSpec((d, GROUP_W), lambda i, j: (0, j))],
        out_specs=pl.BlockSpec((1, tm, GROUP_W), lambda i, j: (j, i, 0)),
        out_shape=jax.ShapeDtypeStruct((ng, n, GROUP_W), f32),
        scratch_shapes=[pltpu.VMEM((tm, d), bf16)],
        compiler_params=_cp(("arbitrary", "arbitrary")),
    )(x_all, g, w_big)


def _rwkv_pre_kernel(ur, uk, uv, ul, sr, sk, sv, sl, mur, muk, muv, mul, w0, a0, wup, aup, gup,
                     out_ref, carry, *, n_prompt_tiles):
    i = pl.program_id(0)
    tt = ur.shape[1]

    @pl.when(i == 0)
    def _():
        carry[...] = jnp.zeros_like(carry)

    rowi = lax.broadcasted_iota(i32, (tt, 1), 0)
    is_sample = i >= n_prompt_tiles
    pos = jnp.where(is_sample, rowi, i * tt + rowi)
    period_mask = jnp.where(is_sample, DEC_SEQ - 1, SEQ - 1)
    seq_start = (pos & period_mask) == 0

    def shift_mix(u_ref, s_ref, mu_ref, c0):
        u = u_ref[0]
        cw = u.shape[1]
        prev = jnp.where(rowi == 0, carry[0:1, c0:c0 + cw], pltpu.roll(u, 1, 0))
        sh = jnp.where(seq_start, s_ref[0], prev)
        carry[0:1, c0:c0 + cw] = u[tt - 1:tt, :]
        return u + (sh - u) * mu_ref[...]

    out_ref[0] = shift_mix(ur, sr, mur, 0)
    out_ref[1] = shift_mix(uk, sk, muk, RWKV_WIDTH)
    out_ref[2] = shift_mix(uv, sv, muv, 2 * RWKV_WIDTH)
    xl = shift_mix(ul, sl, mul, 3 * RWKV_WIDTH)
    lw = _dot(jnp.tanh(xl).astype(bf16), wup[...])
    la = _dot(xl.astype(bf16), aup[...])
    lg = _dot(jax.nn.sigmoid(xl).astype(bf16), gup[...])
    z = -(w0[...] + lw)
    softplus = jnp.maximum(z, 0.0) + jnp.log(1.0 + jnp.exp(-jnp.abs(z)))
    w_log = -softplus - 0.5
    out_ref[3] = -jnp.exp(w_log)
    out_ref[4] = jax.nn.sigmoid(a0[...] + la)
    out_ref[5] = lg


def _rwkv_pre(proj, start_all, mu_all, w0, a0, wup, aup, gup, tt=256):
    n = proj.shape[1]
    npt = NP // tt
    wcols = 3 * RWKV_WIDTH + LORA_PAD
    lblk = 3 * RWKV_WIDTH // LORA_PAD
    u_spec = lambda g: pl.BlockSpec((1, tt, GROUP_W), lambda i, g=g: (g, i, 0))
    s_spec = lambda g: pl.BlockSpec((1, tt, GROUP_W), lambda i, g=g: (jnp.where(i >= npt, 1, 0), 0, g))
    m_spec = lambda g: pl.BlockSpec((1, GROUP_W), lambda i, g=g: (0, g))
    full = lambda shp: pl.BlockSpec(shp, lambda i: tuple(0 for _ in shp))
    return pl.pallas_call(
        functools.partial(_rwkv_pre_kernel, n_prompt_tiles=npt),
        grid=(n // tt,),
        in_specs=[u_spec(0), u_spec(1), u_spec(2),
                  pl.BlockSpec((1, tt, LORA_PAD), lambda i: (6, i, 0)),
                  s_spec(0), s_spec(1), s_spec(2),
                  pl.BlockSpec((1, tt, LORA_PAD), lambda i: (jnp.where(i >= npt, 1, 0), 0, lblk)),
                  m_spec(0), m_spec(1), m_spec(2),
                  pl.BlockSpec((1, LORA_PAD), lambda i: (0, lblk)),
                  full((1, RWKV_WIDTH)), full((1, RWKV_WIDTH)),
                  full((LORA_PAD, RWKV_WIDTH)), full((LORA_PAD, RWKV_WIDTH)), full((LORA_PAD, RWKV_WIDTH))],
        out_specs=pl.BlockSpec((6, tt, GROUP_W), lambda i: (0, i, 0)),
        out_shape=jax.ShapeDtypeStruct((6, n, GROUP_W), f32),
        scratch_shapes=[pltpu.VMEM((8, wcols), f32)],
        compiler_params=_cp(("arbitrary",)),
    )(proj, proj, proj, proj, start_all, start_all, start_all, start_all,
      mu_all, mu_all, mu_all, mu_all, w0, a0, wup, aup, gup)


def _chunk_step(r, k, v, kk, b, lw, m0):
    c = r.shape[0]
    n = r.shape[1]
    row = lax.broadcasted_iota(i32, (c, c), 0)
    col = lax.broadcasted_iota(i32, (c, c), 1)
    tri_incl = col <= row
    strict = col < row
    cum = _dot(tri_incl.astype(f32), lw, HI)
    p_in = jnp.exp(cum)
    p_ex = jnp.exp(cum - lw)
    p_inv = jnp.exp(-cum)
    tot = cum[c - 1:c, :]
    a_mat = jnp.concatenate([kk * p_ex, r * p_in], axis=0)
    b_mat = jnp.concatenate([k * p_inv, b * p_inv], axis=0)
    g = _dot_nt(a_mat, b_mat, HI)
    l_k = jnp.where(strict, g[:c, :c], 0.0)
    l_b = jnp.where(strict, g[:c, c:], 0.0)
    a_rk = jnp.where(tri_incl, g[c:, :c], 0.0)
    a_rb = jnp.where(tri_incl, g[c:, c:], 0.0)
    t_inv = (row == col).astype(f32)
    s = 1
    while s < c:
        sh = s.bit_length() - 1
        off = (((row >> sh) & 1) == 1) & ((col >> sh) == (row >> sh) - 1)
        t_inv = t_inv - _dot(t_inv, _dot(jnp.where(off, l_b, 0.0), t_inv, HI), HI)
        s *= 2
    am = _dot(a_mat, m0, HI)
    u = _dot(t_inv, am[:c] + _dot(l_k, v, HI), HI)
    o = am[c:] + _dot(a_rk, v, HI) - _dot(a_rb, u, HI)
    dec = jnp.exp(tot - cum)
    x = jnp.concatenate([k * dec, b * dec], axis=0)
    y = jnp.concatenate([v, -u], axis=0)
    eye = lax.broadcasted_iota(i32, (n, n), 0) == lax.broadcasted_iota(i32, (n, n), 1)
    dg = jnp.where(eye, jnp.broadcast_to(jnp.exp(tot), (n, n)), 0.0)
    m_new = _dot(dg, m0, HI) + _dot_tn(x, y, HI)
    return o, m_new


def _scan_kernel(in_ref, kk_ref, ka_ref, rk_ref, lnw_ref, lnb_ref, s0_ref, o_ref, sout_ref, m_scr, *, tv):
    c = pl.program_id(2)
    n = HEAD_DIM

    @pl.when(c == 0)
    def _():
        m_scr[...] = s0_ref[0]

    def padded(x):
        if tv < SCAN_CHUNK:
            return jnp.concatenate([x, jnp.zeros((SCAN_CHUNK - tv, x.shape[1]), f32)], axis=0)
        return x

    xr2, xk2, xv2 = padded(in_ref[0]), padded(in_ref[1]), padded(in_ref[2])
    lw2, a2, g2 = padded(in_ref[3]), padded(in_ref[4]), in_ref[5]
    outs = []
    for h in range(2):
        sl = slice(h * n, (h + 1) * n)
        xr, xk, xv, lw, a = xr2[:, sl], xk2[:, sl], xv2[:, sl], lw2[:, sl], a2[:, sl]
        kkr = xk * kk_ref[:, sl]
        nrm = jnp.sqrt(jnp.sum(kkr * kkr, axis=-1, keepdims=True))
        kk = kkr / jnp.maximum(nrm, 1e-12)
        km = xk * (1.0 + (a - 1.0) * ka_ref[:, sl])
        o, m_new = _chunk_step(xr, km, xv, kk, kk * a, lw, m_scr[h])
        m_scr[h] = m_new
        o = o[:tv]
        mean = jnp.mean(o, axis=-1, keepdims=True)
        var = jnp.mean(jnp.square(o - mean), axis=-1, keepdims=True)
        on = (o - mean) * lax.rsqrt(var + GN_EPS) * lnw_ref[:, sl] + lnb_ref[:, sl]
        bonus = jnp.sum(xr[:tv] * km[:tv] * rk_ref[:, sl], axis=-1, keepdims=True) * xv[:tv]
        outs.append((on + bonus) * g2[:, sl])
    o_ref[...] = jnp.concatenate(outs, axis=1)

    @pl.when(c == pl.num_programs(2) - 1)
    def _():
        sout_ref[0] = m_scr[...]


def _scan(scan_in, k_k, k_a, r_k, lnx_w, lnx_b, state0_t, *, n_seq, seq_len, row0):
    tv = min(seq_len, SCAN_CHUNK)
    n_chunks = seq_len // tv
    blk0 = row0 // tv
    par = lambda: pl.BlockSpec((1, 2 * HEAD_DIM), lambda s, p, c: (0, p))
    return pl.pallas_call(
        functools.partial(_scan_kernel, tv=tv),
        grid=(n_seq, N_HEADS // 2, n_chunks),
        in_specs=[pl.BlockSpec((6, tv, 2 * HEAD_DIM), lambda s, p, c: (0, blk0 + s * n_chunks + c, p)),
                  par(), par(), par(), par(), par(),
                  pl.BlockSpec((1, 2, HEAD_DIM, HEAD_DIM), lambda s, p, c: (s, p, 0, 0))],
        out_specs=[pl.BlockSpec((tv, 2 * HEAD_DIM), lambda s, p, c: (s * n_chunks + c, p)),
                   pl.BlockSpec((1, 2, HEAD_DIM, HEAD_DIM), lambda s, p, c: (s, p, 0, 0))],
        out_shape=[jax.ShapeDtypeStruct((n_seq * seq_len, RWKV_WIDTH), f32),
                   jax.ShapeDtypeStruct((n_seq, N_HEADS, HEAD_DIM, HEAD_DIM), f32)],
        scratch_shapes=[pltpu.VMEM((2, HEAD_DIM, HEAD_DIM), f32)],
        compiler_params=_cp(("arbitrary", "arbitrary", "arbitrary")),
    )(scan_in, k_k, k_a, r_k, lnx_w, lnx_b, state0_t)


def _kmean_kernel(k_ref, o_ref):
    o_ref[0] = jnp.mean(k_ref[0], axis=0, keepdims=True)


def _kmean(proj, group, n_blocks):
    return pl.pallas_call(
        _kmean_kernel,
        grid=(n_blocks,),
        in_specs=[pl.BlockSpec((1, MOBA_BLOCK, GROUP_W), lambda i: (group, i, 0))],
        out_specs=pl.BlockSpec((1, 1, GROUP_W), lambda i: (i, 0, 0)),
        out_shape=jax.ShapeDtypeStruct((n_blocks, 1, GROUP_W), f32),
        compiler_params=_cp(("arbitrary",)),
    )(proj)


def _bias_tile(rb_ref, head, d):
    bucket = _t5_bucket(d)
    bias = jnp.zeros(d.shape, f32)
    for kb in range(N_BUCKETS):
        bias = jnp.where(bucket == kb, rb_ref[kb, head], bias)
    return jnp.where(d >= 0, bias, NEG)


def _attn_prompt_kernel(rb_ref, q_ref, k_ref, v_ref, km_ref, o_ref, bias_scr):
    p = pl.program_id(1)
    qi = pl.program_id(2)
    blocks_per_q = MOBA_BLOCK // Q_BLOCK
    own = qi // blocks_per_q
    par = qi % blocks_per_q
    n_blocks = km_ref.shape[1]

    @pl.when(qi == 0)
    def _():
        di = lax.broadcasted_iota(i32, (Q_BLOCK, MOBA_BLOCK), 0) - lax.broadcasted_iota(i32, (Q_BLOCK, MOBA_BLOCK), 1)
        for h in range(2):
            for variant in range(3):
                bias_scr[h * 3 + variant] = _bias_tile(rb_ref, 2 * p + h, di + variant * Q_BLOCK)

    q = q_ref[0]
    lane_head = lax.broadcasted_iota(i32, q.shape, 1) >> 6
    blk = lax.broadcasted_iota(i32, (Q_BLOCK, n_blocks), 1)
    scale = HEAD_DIM ** -0.5
    out = jnp.zeros(q.shape, f32)
    for h in range(2):
        qh = jnp.where(lane_head == h, q, 0.0)
        gate = _dot_nt(qh, km_ref[0], HI)
        sel = _topk_mask(gate, own, MOBA_TOPK)
        qs = (qh * scale).astype(bf16)
        far_bias = rb_ref[N_BUCKETS - 1, 2 * p + h]

        k0 = pl.multiple_of(own * MOBA_BLOCK, MOBA_BLOCK)
        s = _dot_nt(qs, k_ref[0, pl.ds(k0, MOBA_BLOCK), :]) + bias_scr[h * 3 + par]
        m = jnp.max(s, axis=1, keepdims=True)
        pexp = jnp.exp(s - m)
        l = jnp.sum(pexp, axis=1, keepdims=True)
        acc = _dot(pexp.astype(bf16), v_ref[0, pl.ds(k0, MOBA_BLOCK), :])

        def body(j, carry):
            m, l, acc = carry
            kj = pl.multiple_of(j * MOBA_BLOCK, MOBA_BLOCK)
            near = jnp.logical_and(j == own - 1, par == 0)
            bias = jnp.where(near, bias_scr[h * 3 + 2], far_bias)
            chosen = jnp.sum(jnp.where(blk == j, sel, 0.0), axis=1, keepdims=True) > 0.5
            s = jnp.where(chosen, _dot_nt(qs, k_ref[0, pl.ds(kj, MOBA_BLOCK), :]) + bias, NEG)
            m_new = jnp.maximum(m, jnp.max(s, axis=1, keepdims=True))
            alpha = jnp.exp(m - m_new)
            pexp = jnp.exp(s - m_new)
            l = alpha * l + jnp.sum(pexp, axis=1, keepdims=True)
            acc = alpha * acc + _dot(pexp.astype(bf16), v_ref[0, pl.ds(kj, MOBA_BLOCK), :])
            return m_new, l, acc

        m, l, acc = lax.fori_loop(0, own, body, (m, l, acc))
        out = jnp.where(lane_head == h, acc / l, out)
    o_ref[...] = out


def _attn_prompt(rel_bias, proj, kv_bf, kmean, q_group=3):
    nq = SEQ // Q_BLOCK
    n_blocks = SEQ // MOBA_BLOCK
    lanes = 2 * HEAD_DIM
    return pl.pallas_call(
        _attn_prompt_kernel,
        grid=(BATCH, N_HEADS // 2, nq),
        in_specs=[pl.BlockSpec(memory_space=pltpu.SMEM),
                  pl.BlockSpec((1, Q_BLOCK, lanes), lambda b, p, qi: (q_group, b * nq + qi, p)),
                  pl.BlockSpec((1, SEQ, lanes), lambda b, p, qi: (0, b, p)),
                  pl.BlockSpec((1, SEQ, lanes), lambda b, p, qi: (1, b, p)),
                  pl.BlockSpec((1, n_blocks, lanes), lambda b, p, qi: (b, 0, p))],
        out_specs=pl.BlockSpec((Q_BLOCK, lanes), lambda b, p, qi: (b * nq + qi, p)),
        out_shape=jax.ShapeDtypeStruct((NP, ATT_WIDTH), f32),
        scratch_shapes=[pltpu.VMEM((6, Q_BLOCK, MOBA_BLOCK), f32)],
        compiler_params=_cp(("arbitrary", "arbitrary", "arbitrary")),
    )(rel_bias, proj, kv_bf, kv_bf, kmean)


def _attn_sample_kernel(pt_ref, rb_ref, q_ref, kn_ref, vn_ref, *rest):
    pps = PAGES_PER_STEP
    kp, vp = rest[:pps], rest[pps:2 * pps]
    o_ref = rest[2 * pps]
    s_scr, gate_scr, qm_scr, acc_scr, l_scr = rest[2 * pps + 1:]
    step = pl.program_id(1)
    n_steps = N_PAGES // pps
    pages_per_block = MOBA_BLOCK // PAGE_SIZE
    rows = N_HEADS * DEC_SEQ
    row = lax.broadcasted_iota(i32, (rows, 1), 0)
    trow = row & (DEC_SEQ - 1)
    hrow = row >> 3
    blk = lax.broadcasted_iota(i32, (rows, N_PAST_BLOCKS), 1)

    @pl.when(step == 0)
    def _():
        qt = jnp.concatenate([q_ref[...]] * N_HEADS, axis=0)
        lane_head = lax.broadcasted_iota(i32, qt.shape, 1) >> 6
        qm_scr[...] = jnp.where(lane_head == hrow, qt * (HEAD_DIM ** -0.5), 0.0).astype(bf16)
        gate_scr[...] = jnp.zeros_like(gate_scr)

    @pl.when(step < n_steps)
    def _():
        qm = qm_scr[...]
        g = gate_scr[...]
        for i in range(pps):
            page = step * pps + i
            sc = _dot_nt(qm, kp[i][0].astype(bf16))
            s_scr[page] = sc
            g = g + jnp.where(blk == page // pages_per_block, jnp.sum(sc, axis=1, keepdims=True), 0.0)
        gate_scr[...] = g

    @pl.when(step == n_steps - 1)
    def _():
        sel = _topk_mask(gate_scr[...], N_PAST_BLOCKS, MOBA_TOPK)
        d_own = trow - lax.broadcasted_iota(i32, (rows, DEC_SEQ), 1)
        s_own = _dot_nt(qm_scr[...], kn_ref[...].astype(bf16))
        b_own = jnp.zeros(s_own.shape, f32)
        for kb in range(DEC_SEQ):
            b_own = jnp.where(d_own == kb, rb_ref[:, kb:kb + 1], b_own)
        s_own = jnp.where(d_own >= 0, s_own + b_own, NEG)
        far_bias = rb_ref[:, N_BUCKETS - 1:N_BUCKETS]
        def masked(page, bias):
            chosen = jnp.sum(jnp.where(blk == page // pages_per_block, sel, 0.0), axis=1, keepdims=True) > 0.5
            sj = jnp.where(chosen, s_scr[page] + bias, NEG)
            s_scr[page] = sj
            return jnp.max(sj, axis=1, keepdims=True)

        n_far = N_PAGES - pages_per_block
        m = jnp.max(s_own, axis=1, keepdims=True)
        m = lax.fori_loop(0, n_far, lambda j, m: jnp.maximum(m, masked(j, far_bias)), m)
        for page in range(n_far, N_PAGES):
            d_near = (PAST_LEN - page * PAGE_SIZE) + trow - lax.broadcasted_iota(i32, (rows, PAGE_SIZE), 1)
            bucket = _t5_bucket(d_near)
            b_near = jnp.zeros(d_near.shape, f32)
            for kb in range(N_BUCKETS):
                b_near = jnp.where(bucket == kb, rb_ref[:, kb:kb + 1], b_near)
            m = jnp.maximum(m, masked(page, b_near))

        def expo(j, l):
            pj = jnp.exp(s_scr[j] - m)
            s_scr[j] = pj
            return l + jnp.sum(pj, axis=1, keepdims=True)

        p_own = jnp.exp(s_own - m)
        l_scr[...] = lax.fori_loop(0, N_PAGES, expo, jnp.sum(p_own, axis=1, keepdims=True))
        acc = jnp.zeros(acc_scr.shape, f32)
        vn = vn_ref[...]
        for t in range(DEC_SEQ):
            acc = acc + p_own[:, t:t + 1] * vn[t:t + 1, :]
        acc_scr[...] = acc

    @pl.when(step >= n_steps)
    def _():
        acc = acc_scr[...]
        for i in range(pps):
            page = (step - n_steps) * pps + i
            acc = acc + _dot(s_scr[page].astype(bf16), vp[i][0].astype(bf16))
        acc_scr[...] = acc

    @pl.when(step == 2 * n_steps - 1)
    def _():
        a = acc_scr[...] / l_scr[...]
        lane_head = lax.broadcasted_iota(i32, (DEC_SEQ, ATT_WIDTH), 1) >> 6
        out = jnp.zeros((DEC_SEQ, ATT_WIDTH), f32)
        for h in range(N_HEADS):
            out = jnp.where(lane_head == h, a[h * DEC_SEQ:(h + 1) * DEC_SEQ, :], out)
        o_ref[...] = out


def _attn_sample(page_table, rb_rows, q_s, k_new, v_new, pool_k, pool_v):
    pps = PAGES_PER_STEP
    n_steps = N_PAGES // pps
    rows = N_HEADS * DEC_SEQ
    tok = pl.BlockSpec((DEC_SEQ, ATT_WIDTH), lambda b, s, pt: (b, 0))

    def kspec(i):
        return pl.BlockSpec((1, PAGE_SIZE, ATT_WIDTH),
                            lambda b, s, pt, i=i: (pt[b * N_PAGES + jnp.minimum(s, n_steps - 1) * pps + i], 0, 0))

    def vspec(i):
        return pl.BlockSpec((1, PAGE_SIZE, ATT_WIDTH),
                            lambda b, s, pt, i=i: (pt[b * N_PAGES + jnp.maximum(s - n_steps, 0) * pps + i], 0, 0))

    grid_spec = pltpu.PrefetchScalarGridSpec(
        num_scalar_prefetch=1,
        grid=(DEC_BATCH, 2 * n_steps),
        in_specs=[pl.BlockSpec((rows, N_BUCKETS), lambda b, s, pt: (0, 0)), tok, tok, tok]
                 + [kspec(i) for i in range(pps)] + [vspec(i) for i in range(pps)],
        out_specs=pl.BlockSpec((DEC_SEQ, ATT_WIDTH), lambda b, s, pt: (b, 0)),
        scratch_shapes=[pltpu.VMEM((N_PAGES, rows, PAGE_SIZE), f32),
                        pltpu.VMEM((rows, N_PAST_BLOCKS), f32),
                        pltpu.VMEM((rows, ATT_WIDTH), bf16),
                        pltpu.VMEM((rows, ATT_WIDTH), f32),
                        pltpu.VMEM((rows, 1), f32)])
    return pl.pallas_call(
        _attn_sample_kernel,
        grid_spec=grid_spec,
        out_shape=jax.ShapeDtypeStruct((NS, ATT_WIDTH), f32),
        compiler_params=_cp(("arbitrary", "arbitrary")),
    )(page_table.reshape(-1), rb_rows, q_s, k_new, v_new, *([pool_k] * pps), *([pool_v] * pps))


def _outproj_kernel(x_ref, ro_ref, ao_ref, wo_ref, g2_ref, rw_ref, rb_ref, h_ref, xn_ref, gate_ref, mask_ref):
    h = (x_ref[...] + _dot(ro_ref[...].astype(bf16), wo_ref[:RWKV_WIDTH, :])
         + _dot(ao_ref[...].astype(bf16), wo_ref[RWKV_WIDTH:, :]))
    h_ref[...] = h
    xn = _rms(h, g2_ref[...])
    xn_ref[...] = xn.astype(bf16)
    logits = _dot(xn, rw_ref[...], HI) + rb_ref[...]
    sel = _topk_mask(logits, N_EXPERTS, TOP_K)
    mx = jnp.max(jnp.where(sel > 0.5, logits, NEG), axis=1, keepdims=True)
    e = jnp.where(sel > 0.5, jnp.exp(logits - mx), 0.0)
    gate_ref[...] = e / jnp.sum(e, axis=1, keepdims=True)
    mask_ref[...] = sel


def _outproj(x_all, ro, ao, wo_bf, g2, router_w, router_b, tm=256):
    n, d = x_all.shape
    row = lambda w: pl.BlockSpec((tm, w), lambda i: (i, 0))
    full = lambda shp: pl.BlockSpec(shp, lambda i: tuple(0 for _ in shp))
    return pl.pallas_call(
        _outproj_kernel,
        grid=(n // tm,),
        in_specs=[row(d), row(RWKV_WIDTH), row(ATT_WIDTH), full((d, d)), full((1, d)),
                  full((d, N_EXPERTS)), full((1, N_EXPERTS))],
        out_specs=[row(d), row(d), row(N_EXPERTS), row(N_EXPERTS)],
        out_shape=[jax.ShapeDtypeStruct((n, d), f32), jax.ShapeDtypeStruct((n, d), bf16),
                   jax.ShapeDtypeStruct((n, N_EXPERTS), f32), jax.ShapeDtypeStruct((n, N_EXPERTS), f32)],
        compiler_params=_cp(("arbitrary",)),
    )(x_all, ro, ao, wo_bf, g2, router_w, router_b)


def _expert_changed(te_ref, t):
    return jnp.logical_or(t == 0, te_ref[t] != te_ref[jnp.maximum(t - 1, 0)])


def _moe_up_kernel(te_ref, tv_ref, x_ref, wg_ref, wl_ref, bg_ref, bl_ref, act_ref, wg_s, wl_s):
    t = pl.program_id(1)

    @pl.when(_expert_changed(te_ref, t))
    def _():
        wg_s[...] = wg_ref[0].astype(bf16)
        wl_s[...] = wl_ref[0].astype(bf16)

    @pl.when(tv_ref[t] == 1)
    def _():
        x = x_ref[...]
        glu = jnp.minimum(_dot(x, wg_s[...]) + bg_ref[0], SWIGLU_LIMIT)
        lin = jnp.clip(_dot(x, wl_s[...]) + bl_ref[0], -SWIGLU_LIMIT, SWIGLU_LIMIT)
        act_ref[...] = (glu * jax.nn.sigmoid(SWIGLU_ALPHA * glu) * (lin + 1.0)).astype(bf16)

    @pl.when(tv_ref[t] == 0)
    def _():
        act_ref[...] = jnp.zeros_like(act_ref)


def _moe_up(tile_expert, tile_valid, x_sorted, w_gu, b_gu3, tf=512):
    nj = D_FF // tf
    grid_spec = pltpu.PrefetchScalarGridSpec(
        num_scalar_prefetch=2,
        grid=(nj, MOE_TILES),
        in_specs=[pl.BlockSpec((MOE_TM, D_MODEL), lambda j, t, te, tv: (t, 0)),
                  pl.BlockSpec((1, D_MODEL, tf), lambda j, t, te, tv: (te[t], 0, j)),
                  pl.BlockSpec((1, D_MODEL, tf), lambda j, t, te, tv: (te[t], 0, nj + j)),
                  pl.BlockSpec((1, 1, tf), lambda j, t, te, tv: (te[t], 0, j)),
                  pl.BlockSpec((1, 1, tf), lambda j, t, te, tv: (te[t], 0, nj + j))],
        out_specs=pl.BlockSpec((MOE_TM, tf), lambda j, t, te, tv: (t, j)),
        scratch_shapes=[pltpu.VMEM((D_MODEL, tf), bf16), pltpu.VMEM((D_MODEL, tf), bf16)])
    return pl.pallas_call(
        _moe_up_kernel,
        grid_spec=grid_spec,
        out_shape=jax.ShapeDtypeStruct((MOE_ROWS, D_FF), bf16),
        compiler_params=_cp(("arbitrary", "arbitrary")),
    )(tile_expert, tile_valid, x_sorted, w_gu, w_gu, b_gu3, b_gu3)


def _moe_dn_kernel(te_ref, tv_ref, act_ref, wd_ref, bd_ref, gs_ref, y_ref, wd_s):
    t = pl.program_id(1)

    @pl.when(_expert_changed(te_ref, t))
    def _():
        wd_s[...] = wd_ref[0].astype(bf16)

    @pl.when(tv_ref[t] == 1)
    def _():
        y_ref[...] = (_dot(act_ref[...], wd_s[...]) + bd_ref[0]) * gs_ref[...]

    @pl.when(tv_ref[t] == 0)
    def _():
        y_ref[...] = jnp.zeros_like(y_ref)


def _moe_dn(tile_expert, tile_valid, act, w_dn, b_dn3, gate_sorted, tn=1024):
    nj = D_MODEL // tn
    grid_spec = pltpu.PrefetchScalarGridSpec(
        num_scalar_prefetch=2,
        grid=(nj, MOE_TILES),
        in_specs=[pl.BlockSpec((MOE_TM, D_FF), lambda j, t, te, tv: (t, 0)),
                  pl.BlockSpec((1, D_FF, tn), lambda j, t, te, tv: (te[t], 0, j)),
                  pl.BlockSpec((1, 1, tn), lambda j, t, te, tv: (te[t], 0, j)),
                  pl.BlockSpec((MOE_TM, 1), lambda j, t, te, tv: (t, 0))],
        out_specs=pl.BlockSpec((MOE_TM, tn), lambda j, t, te, tv: (t, j)),
        scratch_shapes=[pltpu.VMEM((D_FF, tn), bf16)])
    return pl.pallas_call(
        _moe_dn_kernel,
        grid_spec=grid_spec,
        out_shape=jax.ShapeDtypeStruct((MOE_ROWS, D_MODEL), f32),
        compiler_params=_cp(("arbitrary", "arbitrary")),
    )(tile_expert, tile_valid, act, w_dn, b_dn3, gate_sorted)


def _route(gates, mask):
    n = gates.shape[0]
    mi = mask.astype(i32)
    counts = jnp.sum(mi, axis=0)
    rank = jnp.cumsum(mi, axis=0) - mi
    tiles_per = (counts + MOE_TM - 1) // MOE_TM
    tiles_end = jnp.cumsum(tiles_per)
    dest = ((tiles_end - tiles_per) * MOE_TM)[None, :] + rank
    used = tiles_end[-1]
    t = jnp.arange(MOE_TILES, dtype=i32)
    te = jnp.searchsorted(tiles_end, t, side='right').astype(i32)
    last_e = jnp.max(jnp.where(tiles_per > 0, jnp.arange(N_EXPERTS, dtype=i32), 0))
    tile_valid = (t < used).astype(i32)
    tile_expert = jnp.where(t < used, jnp.minimum(te, N_EXPERTS - 1), last_e)
    _, idx4 = lax.top_k(mask, TOP_K)
    dest4 = jnp.take_along_axis(dest, idx4, axis=1)
    g4 = jnp.take_along_axis(gates, idx4, axis=1)
    tok = jnp.broadcast_to(jnp.arange(n, dtype=i32)[:, None], (n, TOP_K))
    sorted_tok = jnp.zeros((MOE_ROWS,), i32).at[dest4.reshape(-1)].set(tok.reshape(-1))
    gate_sorted = jnp.zeros((MOE_ROWS,), f32).at[dest4.reshape(-1)].set(g4.reshape(-1))
    return tile_expert, tile_valid, sorted_tok, gate_sorted.reshape(-1, 1), dest4


def _final_kernel(h_ref, y4_ref, pe_ref, gn_ref, gw_ref, pp_ref, fn_ref, o_ref):
    h = h_ref[...]
    y = y4_ref[0]
    for kk in range(1, TOP_K):
        y = y + y4_ref[kk]
    h = h + y
    gate = jax.nn.sigmoid(_dot(_rms(h, gn_ref[...]).astype(bf16), gw_ref[...]))
    h = h + gate * _dot(pe_ref[...].astype(bf16), pp_ref[...])
    o_ref[...] = _rms(h, fn_ref[...])


def _final(h1, y4, pe, ple_norm_g, gw_bf, pp_bf, final_norm_g, tm=128):
    n, d = h1.shape
    full = lambda shp: pl.BlockSpec(shp, lambda i: tuple(0 for _ in shp))
    return pl.pallas_call(
        _final_kernel,
        grid=(n // tm,),
        in_specs=[pl.BlockSpec((tm, d), lambda i: (i, 0)),
                  pl.BlockSpec((TOP_K, tm, d), lambda i: (0, i, 0)),
                  pl.BlockSpec((tm, D_PLE), lambda i: (i, 0)),
                  full((1, d)), full((d, d)), full((D_PLE, d)), full((1, d))],
        out_specs=pl.BlockSpec((tm, d), lambda i: (i, 0)),
        out_shape=jax.ShapeDtypeStruct((n, d), f32),
        compiler_params=_cp(("arbitrary",)),
    )(h1, y4, pe, ple_norm_g, gw_bf, pp_bf, final_norm_g)


def kernel(x_prompt, x_sample, cache_k, cache_v, state_wkv, state_shift, page_table, p_prompt, p_sample, norm1_g, w_in, mu_shift, w0, w_up, a0, a_up, g_up, k_k, k_a, r_k, lnx_w, lnx_b, w_out, rel_bias, norm2_g, router_w, router_b, w_gu, b_gu, w_dn, b_dn, ple_norm_g, ple_gate_w, ple_proj, final_norm_g):
    rw = RWKV_WIDTH
    x_all = jnp.concatenate([x_prompt.reshape(NP, D_MODEL), x_sample.reshape(NS, D_MODEL)], axis=0)
    pe = jnp.concatenate([p_prompt[0].reshape(NP, D_PLE), p_sample[0].reshape(NS, D_PLE)], axis=0)

    w = w_in[0]
    w_big = jnp.concatenate([w[:, :3 * rw], w[:, N_SHIFT:],
                             jnp.pad(w[:, 3 * rw:N_SHIFT], ((0, 0), (0, GROUP_W - LORA_WIDTH)))], axis=1).astype(bf16)
    proj = _inproj(x_all, norm1_g, w_big)

    pad_l = lambda a: jnp.pad(a, ((0, 0), (0, LORA_PAD - LORA_WIDTH)))
    start_s = jnp.repeat(state_shift[0], DEC_SEQ, axis=0)
    start_all = jnp.stack([jnp.zeros((NS, 3 * rw + LORA_PAD), f32), pad_l(start_s)])
    mu_all = pad_l(mu_shift)
    rows_pad = lambda a, r0: jnp.pad(a, ((r0, LORA_PAD - r0 - a.shape[0]), (0, 0))).astype(bf16)
    scan_in = _rwkv_pre(proj, start_all, mu_all, w0, a0,
                        rows_pad(w_up[0], 0), rows_pad(a_up[0], DECAY_LORA), rows_pad(g_up[0], DECAY_LORA + ICLR_LORA))
    rk_flat = r_k.reshape(1, rw)
    zero_state = jnp.zeros((BATCH, N_HEADS, HEAD_DIM, HEAD_DIM), f32)
    ro_p, wkv_p = _scan(scan_in, k_k, k_a, rk_flat, lnx_w, lnx_b, zero_state, n_seq=BATCH, seq_len=SEQ, row0=0)
    ro_s, wkv_s = _scan(scan_in, k_k, k_a, rk_flat, lnx_w, lnx_b, jnp.swapaxes(state_wkv[0], -1, -2),
                        n_seq=DEC_BATCH, seq_len=DEC_SEQ, row0=NP)

    kv_bf = proj[4:6].astype(bf16)
    kmean = _kmean(proj, 4, NP // MOBA_BLOCK).reshape(BATCH, SEQ // MOBA_BLOCK, ATT_WIDTH)
    ao_p = _attn_prompt(rel_bias, proj, kv_bf, kmean)
    rb_rows = jnp.repeat(rel_bias.T, DEC_SEQ, axis=0)
    n_pool = cache_k.shape[1]
    ao_s = _attn_sample(page_table, rb_rows, proj[3, NP:], proj[4, NP:], proj[5, NP:],
                        cache_k[0].reshape(n_pool, PAGE_SIZE, ATT_WIDTH), cache_v[0].reshape(n_pool, PAGE_SIZE, ATT_WIDTH))

    ro = jnp.concatenate([ro_p, ro_s], axis=0)
    ao = jnp.concatenate([ao_p, ao_s], axis=0)
    h1, xn2, gates, mask = _outproj(x_all, ro, ao, w_out[0].astype(bf16), norm2_g, router_w[0], router_b)

    tile_expert, tile_valid, sorted_tok, gate_sorted, dest4 = _route(gates, mask)
    x_sorted = jnp.take(xn2, sorted_tok, axis=0)
    act = _moe_up(tile_expert, tile_valid, x_sorted, w_gu[0], b_gu[0].reshape(N_EXPERTS, 1, 2 * D_FF))
    y_sorted = _moe_dn(tile_expert, tile_valid, act, w_dn[0], b_dn[0].reshape(N_EXPERTS, 1, D_MODEL), gate_sorted)
    y4 = jnp.take(y_sorted, dest4.T, axis=0)

    y = _final(h1, y4, pe, ple_norm_g, ple_gate_w[0].astype(bf16), ple_proj[0].astype(bf16),
               final_norm_g.reshape(1, D_MODEL))

    heads = lambda a, b, t: a.reshape(1, b, t, N_HEADS, HEAD_DIM)
    shift_rows = lambda r: jnp.concatenate([proj[0, r], proj[1, r], proj[2, r], proj[6, r, :LORA_WIDTH]], axis=-1)
    last_p = jnp.arange(BATCH) * SEQ + SEQ - 1
    last_s = NP + jnp.arange(DEC_BATCH) * DEC_SEQ + DEC_SEQ - 1
    return (y[:NP].reshape(BATCH, SEQ, D_MODEL),
            y[NP:].reshape(DEC_BATCH, DEC_SEQ, D_MODEL),
            heads(proj[4, :NP], BATCH, SEQ),
            heads(proj[5, :NP], BATCH, SEQ),
            jnp.swapaxes(wkv_p, -1, -2)[None],
            shift_rows(last_p)[None],
            heads(proj[4, NP:], DEC_BATCH, DEC_SEQ),
            heads(proj[5, NP:], DEC_BATCH, DEC_SEQ),
            jnp.swapaxes(wkv_s, -1, -2)[None],
            shift_rows(last_s)[None])
```

```python
import functools
import math

import jax
import jax.numpy as jnp
from jax import lax
from jax.experimental import pallas as pl
from jax.experimental.pallas import tpu as pltpu

f32 = jnp.float32
bf16 = jnp.bfloat16
i32 = jnp.int32
HI = lax.Precision.HIGHEST

D_MODEL = 2048
BATCH = 2
SEQ = 8192
DEC_BATCH = 32
DEC_SEQ = 8
PAST_LEN = 16384
PAGE_SIZE = 128
HEAD_DIM = 64
RWKV_WIDTH = 1024
ATT_WIDTH = 1024
N_HEADS = 16
DECAY_LORA = 64
ICLR_LORA = 64
GATE_LORA = 160
LORA_WIDTH = DECAY_LORA + ICLR_LORA + GATE_LORA
LORA_PAD = 384
GN_EPS = 64e-5
N_SHIFT = 3 * RWKV_WIDTH + LORA_WIDTH
MOBA_BLOCK = 256
MOBA_TOPK = 3
Q_BLOCK = 128
N_BUCKETS = 32
MAX_EXACT = 16
MAX_DISTANCE = 128
N_EXPERTS = 32
TOP_K = 4
D_FF = 2048
SWIGLU_LIMIT = 7.0
SWIGLU_ALPHA = 1.702
D_PLE = 256
RMS_EPS = 1e-6
NEG = -1e30

NP = BATCH * SEQ
NS = DEC_BATCH * DEC_SEQ
NTOK = NP + NS
N_PAGES = PAST_LEN // PAGE_SIZE
N_PAST_BLOCKS = PAST_LEN // MOBA_BLOCK
assert PAST_LEN % MOBA_BLOCK == 0

GROUP_W = 1024
SCAN_CHUNK = 64
SCAN_PREP_CHUNKS = 4
SCAN_PREP_CHAINS = 16
SCAN_APPLY_CHUNKS = 8
MOE_TM = 512
MOE_TILES = -(-(NTOK * TOP_K) // MOE_TM) + N_EXPERTS
MOE_ROWS = MOE_TILES * MOE_TM
PAGES_PER_STEP = 8
ATTN_UNROLL = 4
VMEM_LIMIT = 56 << 20


def _cp(sem, vmem=VMEM_LIMIT):
    return pltpu.CompilerParams(dimension_semantics=sem, vmem_limit_bytes=vmem)


def _dot(a, b, precision=None):
    return jnp.dot(a, b, preferred_element_type=f32, precision=precision)


def _dot_nt(a, b, precision=None):
    return lax.dot_general(a, b, (((1,), (1,)), ((), ())), preferred_element_type=f32, precision=precision)


def _dot_tn(a, b, precision=None):
    return lax.dot_general(a, b, (((0,), (0,)), ((), ())), preferred_element_type=f32, precision=precision)


def _rms(x, g):
    ms = jnp.mean(x * x, axis=-1, keepdims=True)
    return x * lax.rsqrt(ms + RMS_EPS) * g


def _topk_mask(vals, n_valid, k, axis=1):
    idx = lax.broadcasted_iota(i32, vals.shape, axis)
    idxf = idx.astype(f32)
    gm = jnp.where(idx < n_valid, vals, NEG)
    sel = jnp.zeros(vals.shape, f32)
    for _ in range(k):
        mx = jnp.max(gm, axis=axis, keepdims=True)
        cand = jnp.where(gm == mx, jnp.where(gm > 0.5 * NEG, idxf, 1e9), 1e9)
        first = jnp.min(cand, axis=axis, keepdims=True)
        pick = idxf == first
        sel = jnp.where(pick, 1.0, sel)
        gm = jnp.where(pick, NEG, gm)
    return sel


def _t5_bucket(d):
    n = jnp.maximum(d, 0)
    nf = jnp.maximum(n, 1).astype(f32)
    large = MAX_EXACT + (jnp.log(nf / MAX_EXACT) / math.log(MAX_DISTANCE / MAX_EXACT)
                         * (N_BUCKETS - MAX_EXACT)).astype(i32)
    large = jnp.minimum(large, N_BUCKETS - 1)
    return jnp.where(n < MAX_EXACT, n, large)


def _inproj_kernel(x_ref, g_ref, w_ref, o_ref, xn_ref):
    @pl.when(pl.program_id(1) == 0)
    def _():
        xn_ref[...] = _rms(x_ref[...], g_ref[...]).astype(bf16)

    o_ref[0] = _dot(xn_ref[...], w_ref[...])


def _inproj(x_all, g, w_big):
    n, d = x_all.shape
    tm = 640 if n % 640 == 0 else 256
    ng = w_big.shape[1] // GROUP_W
    return pl.pallas_call(
        _inproj_kernel,
        grid=(n // tm, ng),
        in_specs=[pl.BlockSpec((tm, d), lambda i, j: (i, 0)),
                  pl.BlockSpec((1, d), lambda i, j: (0, 0)),
                  pl.BlockSpec((d, GROUP_W), lambda i, j: (0, j))],
        out_specs=pl.BlockSpec((1, tm, GROUP_W), lambda i, j: (j, i, 0)),
        out_shape=jax.ShapeDtypeStruct((ng, n, GROUP_W), f32),
        scratch_shapes=[pltpu.VMEM((tm, d), bf16)],
        compiler_params=_cp(("arbitrary", "arbitrary")),
    )(x_all, g, w_big)


def _rwkv_pre_kernel(ur, uk, uv, ul, sr, sk, sv, sl, mur, muk, muv, mul, w0, a0, wup, aup, gup,
                     out_ref, carry, *, n_prompt_tiles):
    i = pl.program_id(0)
    tt = ur.shape[1]

    @pl.when(i == 0)
    def _():
        carry[...] = jnp.zeros_like(carry)

    rowi = lax.broadcasted_iota(i32, (tt, 1), 0)
    is_sample = i >= n_prompt_tiles
    pos = jnp.where(is_sample, rowi, i * tt + rowi)
    period_mask = jnp.where(is_sample, DEC_SEQ - 1, SEQ - 1)
    seq_start = (pos & period_mask) == 0

    def shift_mix(u_ref, s_ref, mu_ref, c0):
        u = u_ref[0]
        cw = u.shape[1]
        prev = jnp.where(rowi == 0, carry[0:1, c0:c0 + cw], pltpu.roll(u, 1, 0))
        sh = jnp.where(seq_start, s_ref[0], prev)
        carry[0:1, c0:c0 + cw] = u[tt - 1:tt, :]
        return u + (sh - u) * mu_ref[...]

    out_ref[0] = shift_mix(ur, sr, mur, 0)
    out_ref[1] = shift_mix(uk, sk, muk, RWKV_WIDTH)
    out_ref[2] = shift_mix(uv, sv, muv, 2 * RWKV_WIDTH)
    xl = shift_mix(ul, sl, mul, 3 * RWKV_WIDTH)
    lw = _dot(jnp.tanh(xl).astype(bf16), wup[...])
    la = _dot(xl.astype(bf16), aup[...])
    lg = _dot(jax.nn.sigmoid(xl).astype(bf16), gup[...])
    z = -(w0[...] + lw)
    softplus = jnp.maximum(z, 0.0) + jnp.log(1.0 + jnp.exp(-jnp.abs(z)))
    w_log = -softplus - 0.5
    out_ref[3] = -jnp.exp(w_log)
    out_ref[4] = jax.nn.sigmoid(a0[...] + la)
    out_ref[5] = lg


def _rwkv_pre(proj, start_all, mu_all, w0, a0, wup, aup, gup, tt=256):
    n = proj.shape[1]
    npt = NP // tt
    wcols = 3 * RWKV_WIDTH + LORA_PAD
    lblk = 3 * RWKV_WIDTH // LORA_PAD
    u_spec = lambda g: pl.BlockSpec((1, tt, GROUP_W), lambda i, g=g: (g, i, 0))
    s_spec = lambda g: pl.BlockSpec((1, tt, GROUP_W), lambda i, g=g: (jnp.where(i >= npt, 1, 0), 0, g))
    m_spec = lambda g: pl.BlockSpec((1, GROUP_W), lambda i, g=g: (0, g))
    full = lambda shp: pl.BlockSpec(shp, lambda i: tuple(0 for _ in shp))
    return pl.pallas_call(
        functools.partial(_rwkv_pre_kernel, n_prompt_tiles=npt),
        grid=(n // tt,),
        in_specs=[u_spec(0), u_spec(1), u_spec(2),
                  pl.BlockSpec((1, tt, LORA_PAD), lambda i: (6, i, 0)),
                  s_spec(0), s_spec(1), s_spec(2),
                  pl.BlockSpec((1, tt, LORA_PAD), lambda i: (jnp.where(i >= npt, 1, 0), 0, lblk)),
                  m_spec(0), m_spec(1), m_spec(2),
                  pl.BlockSpec((1, LORA_PAD), lambda i: (0, lblk)),
                  full((1, RWKV_WIDTH)), full((1, RWKV_WIDTH)),
                  full((LORA_PAD, RWKV_WIDTH)), full((LORA_PAD, RWKV_WIDTH)), full((LORA_PAD, RWKV_WIDTH))],
        out_specs=pl.BlockSpec((6, tt, GROUP_W), lambda i: (0, i, 0)),
        out_shape=jax.ShapeDtypeStruct((6, n, GROUP_W), f32),
        scratch_shapes=[pltpu.VMEM((8, wcols), f32)],
        compiler_params=_cp(("arbitrary",)),
    )(proj, proj, proj, proj, start_all, start_all, start_all, start_all,
      mu_all, mu_all, mu_all, mu_all, w0, a0, wup, aup, gup)


def _split(a):
    hi = a.astype(bf16)
    return hi, (a - hi.astype(f32)).astype(bf16)


def _mm(a, b, passes, dims=(((1,), (0,)), ((), ()))):
    dot = lambda x, y: lax.dot_general(x, y, dims, preferred_element_type=f32)
    if passes == 1:
        return dot(a.astype(bf16), b.astype(bf16))
    a_hi, a_lo = _split(a)
    b_hi, b_lo = _split(b)
    return dot(a_hi, b_hi) + (dot(a_lo, b_hi) + dot(a_hi, b_lo))


_NT = (((1,), (1,)), ((), ()))
_TN = (((0,), (0,)), ((), ()))


def _prefix_sum_rows(x):
    rows = x.shape[0]
    row = lax.broadcasted_iota(i32, x.shape, 0)
    s = 1
    while s < rows:
        x = x + jnp.where(row >= s, pltpu.roll(x, s, 0), 0.0)
        s *= 2
    return x


def _chunk_prepare(chains):
    c, n = chains[0][0].shape
    row = lax.broadcasted_iota(i32, (c, c), 0)
    col = lax.broadcasted_iota(i32, (c, c), 1)
    tri_incl = col <= row
    strict = col < row
    each = lambda fn, *lists: [fn(*xs) for xs in zip(*lists)]
    rs, ks, vs, kks, bs, lws = (list(x) for x in zip(*chains))
    cums = each(_prefix_sum_rows, lws)
    tots = [cum[c - 1:c, :] for cum in cums]
    p_invs = [jnp.exp(-cum) for cum in cums]
    decs = each(lambda tot, cum: jnp.exp(tot - cum), tots, cums)
    kk_ts = each(lambda kk, cum, lw: kk * jnp.exp(cum - lw), kks, cums, lws)
    r_ts = each(lambda r, cum: r * jnp.exp(cum), rs, cums)
    gs = each(lambda kk_t, r_t, k, b, p_inv: _mm(jnp.concatenate([kk_t, r_t], axis=0),
                                                 jnp.concatenate([k * p_inv, b * p_inv], axis=0), 1, _NT),
              kk_ts, r_ts, ks, bs, p_invs)
    l_bs = [jnp.where(strict, g[:c, c:], 0.0) for g in gs]
    t_invs = [(row == col).astype(f32) for _ in chains]
    s = 1
    while s < c:
        sh = s.bit_length() - 1
        off = (((row >> sh) & 1) == 1) & ((col >> sh) == (row >> sh) - 1)
        tmps = each(lambda l_b, t: _mm(jnp.where(off, l_b, 0.0), t, 1), l_bs, t_invs)
        t_invs = each(lambda t, tmp: t - _mm(t, tmp, 1), t_invs, tmps)
        s *= 2
    lvs = each(lambda g, v: _mm(jnp.concatenate([jnp.where(strict, g[:c, :c], 0.0),
                                                 jnp.where(tri_incl, g[c:, :c], 0.0)], axis=0), v, 3), gs, vs)
    ws = each(lambda t, kk_t, lv: _mm(t, jnp.concatenate([kk_t, lv[:c]], axis=1), 3), t_invs, kk_ts, lvs)
    aws = each(lambda g, w: _mm(jnp.where(tri_incl, g[c:, c:], 0.0), w, 1), gs, ws)
    xws = each(lambda b, dec, w: _mm(b * dec, w, 3, _TN), bs, decs, ws)
    kvs = each(lambda k, dec, v: _mm(k * dec, v, 3, _TN), ks, decs, vs)
    eye = lax.broadcasted_iota(i32, (n, n), 0) == lax.broadcasted_iota(i32, (n, n), 1)
    zs = each(lambda r_t, aw, tot, xw: jnp.concatenate(
        [r_t - aw[:, :n], jnp.where(eye, jnp.broadcast_to(jnp.exp(tot), (n, n)), 0.0) - xw[:, :n]], axis=0),
        r_ts, aws, tots, xws)
    ys = each(lambda lv, aw, kv, xw: jnp.concatenate([lv[c:] - aw[:, n:], kv - xw[:, n:]], axis=0),
              lvs, aws, kvs, xws)
    return zs, ys


def _scan_prep_kernel(in_ref, kk_ref, ka_ref, rk_ref, z_ref, y_ref, bonus_ref, *, tv, heads, cpp):
    n = HEAD_DIM

    def padded(x):
        if tv < SCAN_CHUNK:
            return jnp.concatenate([x, jnp.zeros((SCAN_CHUNK - tv, x.shape[1]), f32)], axis=0)
        return x

    chains = []
    for ci in range(cpp):
        rows = slice(ci * tv, (ci + 1) * tv)
        xr2, xk2, xv2, lw2, a2 = (padded(in_ref[g, rows, :]) for g in range(5))
        bonus = []
        for h in range(heads):
            sl = slice(h * n, (h + 1) * n)
            xr, xk, xv, lw, a = xr2[:, sl], xk2[:, sl], xv2[:, sl], lw2[:, sl], a2[:, sl]
            kkr = xk * kk_ref[:, sl]
            nrm = jnp.sqrt(jnp.sum(kkr * kkr, axis=-1, keepdims=True))
            kk = kkr / jnp.maximum(nrm, 1e-12)
            km = xk * (1.0 + (a - 1.0) * ka_ref[:, sl])
            chains.append((xr, km, xv, kk, kk * a, lw))
            bonus.append(jnp.sum(xr[:tv] * km[:tv] * rk_ref[:, sl], axis=-1, keepdims=True) * xv[:tv])
        bonus_ref[rows, :] = jnp.concatenate(bonus, axis=1)
    zs, ys = _chunk_prepare(chains)
    for ci in range(cpp):
        for p in range(heads // 2):
            i0 = ci * heads + 2 * p
            z_ref[ci, p] = jnp.concatenate(zs[i0:i0 + 2], axis=1)
            y_ref[ci, p] = jnp.concatenate(ys[i0:i0 + 2], axis=1)


def _scan_apply_kernel(z_ref, y_ref, bonus_ref, g_ref, lnw_ref, lnb_ref, s0_ref, o_ref, sout_ref, m_scr, *, tv, cps):
    c = pl.program_id(2)
    n = HEAD_DIM

    @pl.when(c == 0)
    def _():
        m_scr[...] = s0_ref[0]

    m = [m_scr[0], m_scr[1]]
    for ci in range(cps):
        z2, y2 = z_ref[ci, 0], y_ref[ci, 0]
        rows = slice(ci * tv, (ci + 1) * tv)
        outs = []
        for h in range(2):
            sl = slice(h * n, (h + 1) * n)
            res = _mm(z2[:, sl], m[h], 3) + y2[:, sl]
            m[h] = res[SCAN_CHUNK:]
            o = res[:tv]
            mean = jnp.mean(o, axis=-1, keepdims=True)
            var = jnp.mean(jnp.square(o - mean), axis=-1, keepdims=True)
            outs.append((o - mean) * lax.rsqrt(var + GN_EPS) * lnw_ref[:, sl] + lnb_ref[:, sl])
        o_ref[rows, :] = (jnp.concatenate(outs, axis=1) + bonus_ref[rows, :]) * g_ref[0, rows, :]
    m_scr[0] = m[0]
    m_scr[1] = m[1]

    @pl.when(c == pl.num_programs(2) - 1)
    def _():
        sout_ref[0] = m_scr[...]


def _scan(scan_in, k_k, k_a, r_k, lnx_w, lnx_b, state0_t, *, n_seq, seq_len, row0):
    tv = min(seq_len, SCAN_CHUNK)
    n_chunks = seq_len // tv
    cpp = min(SCAN_PREP_CHUNKS, n_chunks)
    heads = min(SCAN_PREP_CHAINS // cpp, N_HEADS)
    psteps = n_chunks // cpp
    prow = cpp * tv
    lanes = heads * HEAD_DIM
    n_rows = n_seq * seq_len
    pair = (2 * SCAN_CHUNK, 2 * HEAD_DIM)
    par4 = lambda: pl.BlockSpec((1, lanes), lambda s, p, c: (0, p))
    zy_shape = jax.ShapeDtypeStruct((n_seq * n_chunks, N_HEADS // 2) + pair, f32)
    zy_spec = pl.BlockSpec((cpp, heads // 2) + pair, lambda s, p, c: (s * psteps + c, p, 0, 0))
    z, y, bonus = pl.pallas_call(
        functools.partial(_scan_prep_kernel, tv=tv, heads=heads, cpp=cpp),
        grid=(n_seq, N_HEADS // heads, psteps),
        in_specs=[pl.BlockSpec((5, prow, lanes), lambda s, p, c: (0, row0 // prow + s * psteps + c, p)),
                  par4(), par4(), par4()],
        out_specs=[zy_spec, zy_spec, pl.BlockSpec((prow, lanes), lambda s, p, c: (s * psteps + c, p))],
        out_shape=[zy_shape, zy_shape, jax.ShapeDtypeStruct((n_rows, RWKV_WIDTH), f32)],
        compiler_params=_cp(("arbitrary", "arbitrary", "arbitrary")),
    )(scan_in, k_k, k_a, r_k)

    cps = min(SCAN_APPLY_CHUNKS, n_chunks)
    steps = n_chunks // cps
    rows = cps * tv
    par2 = lambda: pl.BlockSpec((1, 2 * HEAD_DIM), lambda s, p, c: (0, p))
    zy2 = pl.BlockSpec((cps, 1) + pair, lambda s, p, c: (s * steps + c, p, 0, 0))
    row_spec = pl.BlockSpec((rows, 2 * HEAD_DIM), lambda s, p, c: (s * steps + c, p))
    state_spec = pl.BlockSpec((1, 2, HEAD_DIM, HEAD_DIM), lambda s, p, c: (s, p, 0, 0))
    return pl.pallas_call(
        functools.partial(_scan_apply_kernel, tv=tv, cps=cps),
        grid=(n_seq, N_HEADS // 2, steps),
        in_specs=[zy2, zy2, row_spec,
                  pl.BlockSpec((1, rows, 2 * HEAD_DIM), lambda s, p, c: (5, row0 // rows + s * steps + c, p)),
                  par2(), par2(), state_spec],
        out_specs=[row_spec, state_spec],
        out_shape=[jax.ShapeDtypeStruct((n_rows, RWKV_WIDTH), f32),
                   jax.ShapeDtypeStruct((n_seq, N_HEADS, HEAD_DIM, HEAD_DIM), f32)],
        scratch_shapes=[pltpu.VMEM((2, HEAD_DIM, HEAD_DIM), f32)],
        compiler_params=_cp(("arbitrary", "arbitrary", "arbitrary")),
    )(z, y, bonus, scan_in, lnx_w, lnx_b, state0_t)


def _kmean_kernel(k_ref, o_ref):
    o_ref[0] = jnp.mean(k_ref[0], axis=0, keepdims=True)


def _kmean(proj, group, n_blocks):
    return pl.pallas_call(
        _kmean_kernel,
        grid=(n_blocks,),
        in_specs=[pl.BlockSpec((1, MOBA_BLOCK, GROUP_W), lambda i: (group, i, 0))],
        out_specs=pl.BlockSpec((1, 1, GROUP_W), lambda i: (i, 0, 0)),
        out_shape=jax.ShapeDtypeStruct((n_blocks, 1, GROUP_W), f32),
        compiler_params=_cp(("arbitrary",)),
    )(proj)


def _bias_tile(rb_ref, head, d):
    bucket = _t5_bucket(d)
    bias = jnp.zeros(d.shape, f32)
    for kb in range(N_BUCKETS):
        bias = jnp.where(bucket == kb, rb_ref[kb, head], bias)
    return jnp.where(d >= 0, bias, NEG)


def _fold8(x, op):
    parts = [x[i * 8:(i + 1) * 8] for i in range(x.shape[0] // 8)]
    while len(parts) > 1:
        parts = [op(parts[i], parts[i + 1]) for i in range(0, len(parts), 2)]
    return parts[0]


def _attn_prompt_kernel(rb_ref, q_ref, k_ref, vt_ref, km_ref, o_ref, bias_scr, s_scr, sel_scr):
    p = pl.program_id(1)
    qi = pl.program_id(2)
    blocks_per_q = MOBA_BLOCK // Q_BLOCK
    own = qi // blocks_per_q
    par = qi % blocks_per_q
    n_blocks = km_ref.shape[1]
    unroll = ATTN_UNROLL
    own_slot = s_scr.shape[0] - 1
    prev_slot = own_slot - 1
    n_far = jnp.maximum(own - 1, 0)
    n_groups = (n_far + unroll - 1) // unroll

    @pl.when(qi == 0)
    def _():
        di = lax.broadcasted_iota(i32, (MOBA_BLOCK, Q_BLOCK), 1) - lax.broadcasted_iota(i32, (MOBA_BLOCK, Q_BLOCK), 0)
        for h in range(2):
            for variant in range(3):
                bias_scr[h * 3 + variant] = _bias_tile(rb_ref, 2 * p + h, di + variant * Q_BLOCK)

    q_t = q_ref[0].T
    row_head = lax.broadcasted_iota(i32, q_t.shape, 0) >> 6
    scale = HEAD_DIM ** -0.5
    out_t = jnp.zeros(q_t.shape, f32)
    prev = jnp.maximum(own - 1, 0)
    for h in range(2):
        qh = jnp.where(row_head == h, q_t, 0.0)
        sel_scr[...] = _topk_mask(_dot(km_ref[0], qh, HI), own, MOBA_TOPK, axis=0)
        qs = (qh * scale).astype(bf16)
        far_bias = rb_ref[N_BUCKETS - 1, 2 * p + h]

        def scores(j):
            return _dot(k_ref[pl.ds(pl.multiple_of(j * MOBA_BLOCK, MOBA_BLOCK), MOBA_BLOCK), :], qs)

        s_own = scores(own) + bias_scr[h * 3 + par]
        s_scr[own_slot] = s_own
        chosen_prev = jnp.where(own >= 1, sel_scr[pl.ds(prev, 1), :], 0.0) > 0.5
        bias_prev = jnp.where(par == 0, bias_scr[h * 3 + 2], far_bias)
        s_prev = jnp.where(chosen_prev, scores(prev) + bias_prev, NEG)
        s_scr[prev_slot] = s_prev
        mrun = jnp.maximum(_fold8(s_own, jnp.maximum), _fold8(s_prev, jnp.maximum))

        def pass1(g, mrun):
            for u in range(unroll):
                j = g * unroll + u
                jc = jnp.minimum(j, n_blocks - 1)
                chosen = jnp.where(j < n_far, sel_scr[pl.ds(jc, 1), :], 0.0) > 0.5
                s = jnp.where(chosen, scores(jc) + far_bias, NEG)
                s_scr[j] = s
                mrun = jnp.maximum(mrun, _fold8(s, jnp.maximum))
            return mrun

        mrun = lax.fori_loop(0, n_groups, pass1, mrun)
        m = jnp.max(mrun, axis=0, keepdims=True)

        def weighted(slot, j):
            pj = jnp.exp(s_scr[slot] - m)
            return _fold8(pj, jnp.add), _dot(vt_ref[j], pj.astype(bf16))

        l_own, a_own = weighted(own_slot, own)
        l_prev, a_prev = weighted(prev_slot, prev)

        def pass2(g, carry):
            lrun, acc = carry
            for u in range(unroll):
                j = g * unroll + u
                lj, aj = weighted(j, jnp.minimum(j, n_blocks - 1))
                lrun = lrun + lj
                acc = acc + aj
            return lrun, acc

        lrun, acc = lax.fori_loop(0, n_groups, pass2, (l_own + l_prev, a_own + a_prev))
        out_t = jnp.where(row_head == h, acc / jnp.sum(lrun, axis=0, keepdims=True), out_t)
    o_ref[...] = out_t.T


def _attn_prompt(rel_bias, proj, k_bf, vt_bf, kmean, q_group=3):
    nq = SEQ // Q_BLOCK
    n_blocks = SEQ // MOBA_BLOCK
    lanes = 2 * HEAD_DIM
    return pl.pallas_call(
        _attn_prompt_kernel,
        grid=(BATCH, N_HEADS // 2, nq),
        in_specs=[pl.BlockSpec(memory_space=pltpu.SMEM),
                  pl.BlockSpec((1, Q_BLOCK, lanes), lambda b, p, qi: (q_group, b * nq + qi, p)),
                  pl.BlockSpec((SEQ, lanes), lambda b, p, qi: (b, p)),
                  pl.BlockSpec((n_blocks, lanes, MOBA_BLOCK), lambda b, p, qi: (b, p, 0)),
                  pl.BlockSpec((1, n_blocks, lanes), lambda b, p, qi: (b, 0, p))],
        out_specs=pl.BlockSpec((Q_BLOCK, lanes), lambda b, p, qi: (b * nq + qi, p)),
        out_shape=jax.ShapeDtypeStruct((NP, ATT_WIDTH), f32),
        scratch_shapes=[pltpu.VMEM((6, MOBA_BLOCK, Q_BLOCK), f32),
                        pltpu.VMEM((n_blocks + ATTN_UNROLL + 2, MOBA_BLOCK, Q_BLOCK), f32),
                        pltpu.VMEM((n_blocks, Q_BLOCK), f32)],
        compiler_params=_cp(("arbitrary", "arbitrary", "arbitrary")),
    )(rel_bias, proj, k_bf, vt_bf, kmean)


def _attn_sample_kernel(pt_ref, rb_ref, q_ref, kn_ref, vn_ref, *rest):
    pps = PAGES_PER_STEP
    kp, vp = rest[:pps], rest[pps:2 * pps]
    o_ref = rest[2 * pps]
    s_scr, gate_scr, qm_scr, qc_scr, acc_scr, l_scr = rest[2 * pps + 1:]
    step = pl.program_id(1)
    hd = HEAD_DIM
    head_rows = lambda h: slice(h * DEC_SEQ, (h + 1) * DEC_SEQ)
    n_steps = N_PAGES // pps
    pages_per_block = MOBA_BLOCK // PAGE_SIZE
    rows = N_HEADS * DEC_SEQ
    row = lax.broadcasted_iota(i32, (rows, 1), 0)
    trow = row & (DEC_SEQ - 1)
    hrow = row >> 3
    blk = lax.broadcasted_iota(i32, (rows, N_PAST_BLOCKS), 1)

    @pl.when(step == 0)
    def _():
        qt = jnp.concatenate([q_ref[...]] * N_HEADS, axis=0)
        lane_head = lax.broadcasted_iota(i32, qt.shape, 1) >> 6
        qm_scr[...] = jnp.where(lane_head == hrow, qt * (HEAD_DIM ** -0.5), 0.0).astype(bf16)
        q = q_ref[...] * (HEAD_DIM ** -0.5)
        for h in range(N_HEADS):
            qc_scr[head_rows(h), :] = q[:, h * hd:(h + 1) * hd]
        gate_scr[...] = jnp.zeros_like(gate_scr)

    @pl.when(step < n_steps)
    def _():
        g = gate_scr[...]
        qh = [qc_scr[head_rows(h), :].astype(bf16) for h in range(N_HEADS)]
        for i in range(pps):
            page = step * pps + i
            sc = jnp.concatenate([_dot(qh[h], kp[i][0, 0, h].astype(bf16)) for h in range(N_HEADS)],
                                 axis=0)
            s_scr[page] = sc
            g = g + jnp.where(blk == page // pages_per_block, jnp.sum(sc, axis=1, keepdims=True), 0.0)
        gate_scr[...] = g

    @pl.when(step == n_steps - 1)
    def _():
        sel = _topk_mask(gate_scr[...], N_PAST_BLOCKS, MOBA_TOPK)
        d_own = trow - lax.broadcasted_iota(i32, (rows, DEC_SEQ), 1)
        s_own = _dot_nt(qm_scr[...], kn_ref[...].astype(bf16))
        b_own = jnp.zeros(s_own.shape, f32)
        for kb in range(DEC_SEQ):
            b_own = jnp.where(d_own == kb, rb_ref[:, kb:kb + 1], b_own)
        s_own = jnp.where(d_own >= 0, s_own + b_own, NEG)
        far_bias = rb_ref[:, N_BUCKETS - 1:N_BUCKETS]
        def masked(page, bias):
            chosen = jnp.sum(jnp.where(blk == page // pages_per_block, sel, 0.0), axis=1, keepdims=True) > 0.5
            sj = jnp.where(chosen, s_scr[page] + bias, NEG)
            s_scr[page] = sj
            return jnp.max(sj, axis=1, keepdims=True)

        n_far = N_PAGES - pages_per_block
        m = jnp.max(s_own, axis=1, keepdims=True)
        m = lax.fori_loop(0, n_far, lambda j, m: jnp.maximum(m, masked(j, far_bias)), m)
        for page in range(n_far, N_PAGES):
            d_near = (PAST_LEN - page * PAGE_SIZE) + trow - lax.broadcasted_iota(i32, (rows, PAGE_SIZE), 1)
            bucket = _t5_bucket(d_near)
            b_near = jnp.zeros(d_near.shape, f32)
            for kb in range(N_BUCKETS):
                b_near = jnp.where(bucket == kb, rb_ref[:, kb:kb + 1], b_near)
            m = jnp.maximum(m, masked(page, b_near))

        def expo(j, l):
            pj = jnp.exp(s_scr[j] - m)
            s_scr[j] = pj
            return l + jnp.sum(pj, axis=1, keepdims=True)

        p_own = jnp.exp(s_own - m)
        l_scr[...] = lax.fori_loop(0, N_PAGES, expo, jnp.sum(p_own, axis=1, keepdims=True))
        acc = jnp.zeros((rows, ATT_WIDTH), f32)
        vn = vn_ref[...]
        for t in range(DEC_SEQ):
            acc = acc + p_own[:, t:t + 1] * vn[t:t + 1, :]
        for h in range(N_HEADS):
            acc_scr[head_rows(h), :] = acc[head_rows(h), h * hd:(h + 1) * hd]

    @pl.when(step >= n_steps)
    def _():
        acc = acc_scr[...]
        for i in range(pps):
            page = (step - n_steps) * pps + i
            acc = acc + jnp.concatenate([_dot_nt(s_scr[page, head_rows(h), :].astype(bf16),
                                                 vp[i][0, 0, h].astype(bf16))
                                         for h in range(N_HEADS)], axis=0)
        acc_scr[...] = acc

    @pl.when(step == 2 * n_steps - 1)
    def _():
        a = acc_scr[...] / l_scr[...]
        o_ref[...] = jnp.concatenate([a[head_rows(h), :] for h in range(N_HEADS)], axis=1)


def _attn_sample(page_table, rb_rows, q_s, k_new, v_new, pool_k, pool_v):
    pps = PAGES_PER_STEP
    n_steps = N_PAGES // pps
    rows = N_HEADS * DEC_SEQ
    tok = pl.BlockSpec((DEC_SEQ, ATT_WIDTH), lambda b, s, pt: (b, 0))
    page_block = (1, 1, N_HEADS, HEAD_DIM, PAGE_SIZE)

    def kspec(i):
        return pl.BlockSpec(page_block, lambda b, s, pt, i=i:
                            (0, pt[b * N_PAGES + jnp.minimum(s, n_steps - 1) * pps + i], 0, 0, 0))

    def vspec(i):
        return pl.BlockSpec(page_block, lambda b, s, pt, i=i:
                            (0, pt[b * N_PAGES + jnp.maximum(s - n_steps, 0) * pps + i], 0, 0, 0))

    grid_spec = pltpu.PrefetchScalarGridSpec(
        num_scalar_prefetch=1,
        grid=(DEC_BATCH, 2 * n_steps),
        in_specs=[pl.BlockSpec((rows, N_BUCKETS), lambda b, s, pt: (0, 0)), tok, tok, tok]
                 + [kspec(i) for i in range(pps)] + [vspec(i) for i in range(pps)],
        out_specs=pl.BlockSpec((DEC_SEQ, ATT_WIDTH), lambda b, s, pt: (b, 0)),
        scratch_shapes=[pltpu.VMEM((N_PAGES, rows, PAGE_SIZE), f32),
                        pltpu.VMEM((rows, N_PAST_BLOCKS), f32),
                        pltpu.VMEM((rows, ATT_WIDTH), bf16),
                        pltpu.VMEM((rows, HEAD_DIM), f32),
                        pltpu.VMEM((rows, HEAD_DIM), f32),
                        pltpu.VMEM((rows, 1), f32)])
    return pl.pallas_call(
        _attn_sample_kernel,
        grid_spec=grid_spec,
        out_shape=jax.ShapeDtypeStruct((NS, ATT_WIDTH), f32),
        compiler_params=_cp(("arbitrary", "arbitrary")),
    )(page_table.reshape(-1), rb_rows, q_s, k_new, v_new, *([pool_k] * pps), *([pool_v] * pps))


def _outproj_kernel(x_ref, ro_ref, ao_ref, wo_ref, g2_ref, rw_ref, rb_ref, h_ref, xn_ref, gate_ref, mask_ref):
    h = (x_ref[...] + _dot(ro_ref[...].astype(bf16), wo_ref[:RWKV_WIDTH, :])
         + _dot(ao_ref[...].astype(bf16), wo_ref[RWKV_WIDTH:, :]))
    h_ref[...] = h
    xn = _rms(h, g2_ref[...])
    xn_ref[...] = xn.astype(bf16)
    logits = _dot(xn, rw_ref[...], HI) + rb_ref[...]
    sel = _topk_mask(logits, N_EXPERTS, TOP_K)
    mx = jnp.max(jnp.where(sel > 0.5, logits, NEG), axis=1, keepdims=True)
    e = jnp.where(sel > 0.5, jnp.exp(logits - mx), 0.0)
    gate_ref[...] = e / jnp.sum(e, axis=1, keepdims=True)
    mask_ref[...] = sel


def _outproj(x_all, ro, ao, wo_bf, g2, router_w, router_b, tm=256):
    n, d = x_all.shape
    row = lambda w: pl.BlockSpec((tm, w), lambda i: (i, 0))
    full = lambda shp: pl.BlockSpec(shp, lambda i: tuple(0 for _ in shp))
    return pl.pallas_call(
        _outproj_kernel,
        grid=(n // tm,),
        in_specs=[row(d), row(RWKV_WIDTH), row(ATT_WIDTH), full((d, d)), full((1, d)),
                  full((d, N_EXPERTS)), full((1, N_EXPERTS))],
        out_specs=[row(d), row(d), row(N_EXPERTS), row(N_EXPERTS)],
        out_shape=[jax.ShapeDtypeStruct((n, d), f32), jax.ShapeDtypeStruct((n, d), bf16),
                   jax.ShapeDtypeStruct((n, N_EXPERTS), f32), jax.ShapeDtypeStruct((n, N_EXPERTS), f32)],
        compiler_params=_cp(("arbitrary",)),
    )(x_all, ro, ao, wo_bf, g2, router_w, router_b)


def _expert_changed(te_ref, t):
    return jnp.logical_or(t == 0, te_ref[t] != te_ref[jnp.maximum(t - 1, 0)])


def _moe_up_kernel(te_ref, tv_ref, x_ref, wg_ref, wl_ref, bg_ref, bl_ref, act_ref, wg_s, wl_s):
    t = pl.program_id(1)

    @pl.when(_expert_changed(te_ref, t))
    def _():
        wg_s[...] = wg_ref[0].astype(bf16)
        wl_s[...] = wl_ref[0].astype(bf16)

    @pl.when(tv_ref[t] == 1)
    def _():
        x = x_ref[...]
        glu = jnp.minimum(_dot(x, wg_s[...]) + bg_ref[0], SWIGLU_LIMIT)
        lin = jnp.clip(_dot(x, wl_s[...]) + bl_ref[0], -SWIGLU_LIMIT, SWIGLU_LIMIT)
        act_ref[...] = (glu * jax.nn.sigmoid(SWIGLU_ALPHA * glu) * (lin + 1.0)).astype(bf16)

    @pl.when(tv_ref[t] == 0)
    def _():
        act_ref[...] = jnp.zeros_like(act_ref)


def _moe_up(tile_expert, tile_valid, x_sorted, w_gu, b_gu3, tf=512):
    nj = D_FF // tf
    grid_spec = pltpu.PrefetchScalarGridSpec(
        num_scalar_prefetch=2,
        grid=(nj, MOE_TILES),
        in_specs=[pl.BlockSpec((MOE_TM, D_MODEL), lambda j, t, te, tv: (t, 0)),
                  pl.BlockSpec((1, D_MODEL, tf), lambda j, t, te, tv: (te[t], 0, j)),
                  pl.BlockSpec((1, D_MODEL, tf), lambda j, t, te, tv: (te[t], 0, nj + j)),
                  pl.BlockSpec((1, 1, tf), lambda j, t, te, tv: (te[t], 0, j)),
                  pl.BlockSpec((1, 1, tf), lambda j, t, te, tv: (te[t], 0, nj + j))],
        out_specs=pl.BlockSpec((MOE_TM, tf), lambda j, t, te, tv: (t, j)),
        scratch_shapes=[pltpu.VMEM((D_MODEL, tf), bf16), pltpu.VMEM((D_MODEL, tf), bf16)])
    return pl.pallas_call(
        _moe_up_kernel,
        grid_spec=grid_spec,
        out_shape=jax.ShapeDtypeStruct((MOE_ROWS, D_FF), bf16),
        compiler_params=_cp(("arbitrary", "arbitrary")),
    )(tile_expert, tile_valid, x_sorted, w_gu, w_gu, b_gu3, b_gu3)


def _moe_dn_kernel(te_ref, tv_ref, act_ref, wd_ref, bd_ref, gs_ref, y_ref, wd_s):
    t = pl.program_id(1)

    @pl.when(_expert_changed(te_ref, t))
    def _():
        wd_s[...] = wd_ref[0].astype(bf16)

    @pl.when(tv_ref[t] == 1)
    def _():
        y_ref[...] = (_dot(act_ref[...], wd_s[...]) + bd_ref[0]) * gs_ref[...]

    @pl.when(tv_ref[t] == 0)
    def _():
        y_ref[...] = jnp.zeros_like(y_ref)


def _moe_dn(tile_expert, tile_valid, act, w_dn, b_dn3, gate_sorted, tn=1024):
    nj = D_MODEL // tn
    grid_spec = pltpu.PrefetchScalarGridSpec(
        num_scalar_prefetch=2,
        grid=(nj, MOE_TILES),
        in_specs=[pl.BlockSpec((MOE_TM, D_FF), lambda j, t, te, tv: (t, 0)),
                  pl.BlockSpec((1, D_FF, tn), lambda j, t, te, tv: (te[t], 0, j)),
                  pl.BlockSpec((1, 1, tn), lambda j, t, te, tv: (te[t], 0, j)),
                  pl.BlockSpec((MOE_TM, 1), lambda j, t, te, tv: (t, 0))],
        out_specs=pl.BlockSpec((MOE_TM, tn), lambda j, t, te, tv: (t, j)),
        scratch_shapes=[pltpu.VMEM((D_FF, tn), bf16)])
    return pl.pallas_call(
        _moe_dn_kernel,
        grid_spec=grid_spec,
        out_shape=jax.ShapeDtypeStruct((MOE_ROWS, D_MODEL), f32),
        compiler_params=_cp(("arbitrary", "arbitrary")),
    )(tile_expert, tile_valid, act, w_dn, b_dn3, gate_sorted)


def _route(gates, mask):
    n = gates.shape[0]
    mi = mask.astype(i32)
    counts = jnp.sum(mi, axis=0)
    rank = jnp.cumsum(mi, axis=0) - mi
    tiles_per = (counts + MOE_TM - 1) // MOE_TM
    tiles_end = jnp.cumsum(tiles_per)
    dest = ((tiles_end - tiles_per) * MOE_TM)[None, :] + rank
    used = tiles_end[-1]
    t = jnp.arange(MOE_TILES, dtype=i32)
    te = jnp.searchsorted(tiles_end, t, side='right').astype(i32)
    last_e = jnp.max(jnp.where(tiles_per > 0, jnp.arange(N_EXPERTS, dtype=i32), 0))
    tile_valid = (t < used).astype(i32)
    tile_expert = jnp.where(t < used, jnp.minimum(te, N_EXPERTS - 1), last_e)
    _, idx4 = lax.top_k(mask, TOP_K)
    dest4 = jnp.take_along_axis(dest, idx4, axis=1)
    g4 = jnp.take_along_axis(gates, idx4, axis=1)
    tok = jnp.broadcast_to(jnp.arange(n, dtype=i32)[:, None], (n, TOP_K))
    sorted_tok = jnp.zeros((MOE_ROWS,), i32).at[dest4.reshape(-1)].set(tok.reshape(-1))
    gate_sorted = jnp.zeros((MOE_ROWS,), f32).at[dest4.reshape(-1)].set(g4.reshape(-1))
    return tile_expert, tile_valid, sorted_tok, gate_sorted.reshape(-1, 1), dest4


def _final_kernel(h_ref, y4_ref, pe_ref, gn_ref, gw_ref, pp_ref, fn_ref, o_ref):
    h = h_ref[...]
    y = y4_ref[0]
    for kk in range(1, TOP_K):
        y = y + y4_ref[kk]
    h = h + y
    gate = jax.nn.sigmoid(_dot(_rms(h, gn_ref[...]).astype(bf16), gw_ref[...]))
    h = h + gate * _dot(pe_ref[...].astype(bf16), pp_ref[...])
    o_ref[...] = _rms(h, fn_ref[...])


def _final(h1, y4, pe, ple_norm_g, gw_bf, pp_bf, final_norm_g, tm=128):
    n, d = h1.shape
    full = lambda shp: pl.BlockSpec(shp, lambda i: tuple(0 for _ in shp))
    return pl.pallas_call(
        _final_kernel,
        grid=(n // tm,),
        in_specs=[pl.BlockSpec((tm, d), lambda i: (i, 0)),
                  pl.BlockSpec((TOP_K, tm, d), lambda i: (0, i, 0)),
                  pl.BlockSpec((tm, D_PLE), lambda i: (i, 0)),
                  full((1, d)), full((d, d)), full((D_PLE, d)), full((1, d))],
        out_specs=pl.BlockSpec((tm, d), lambda i: (i, 0)),
        out_shape=jax.ShapeDtypeStruct((n, d), f32),
        compiler_params=_cp(("arbitrary",)),
    )(h1, y4, pe, ple_norm_g, gw_bf, pp_bf, final_norm_g)


def kernel(x_prompt, x_sample, cache_k, cache_v, state_wkv, state_shift, page_table, p_prompt, p_sample, norm1_g, w_in, mu_shift, w0, w_up, a0, a_up, g_up, k_k, k_a, r_k, lnx_w, lnx_b, w_out, rel_bias, norm2_g, router_w, router_b, w_gu, b_gu, w_dn, b_dn, ple_norm_g, ple_gate_w, ple_proj, final_norm_g):
    rw = RWKV_WIDTH
    x_all = jnp.concatenate([x_prompt.reshape(NP, D_MODEL), x_sample.reshape(NS, D_MODEL)], axis=0)
    pe = jnp.concatenate([p_prompt[0].reshape(NP, D_PLE), p_sample[0].reshape(NS, D_PLE)], axis=0)

    w = w_in[0]
    w_big = jnp.concatenate([w[:, :3 * rw], w[:, N_SHIFT:],
                             jnp.pad(w[:, 3 * rw:N_SHIFT], ((0, 0), (0, GROUP_W - LORA_WIDTH)))], axis=1).astype(bf16)
    proj = _inproj(x_all, norm1_g, w_big)

    pad_l = lambda a: jnp.pad(a, ((0, 0), (0, LORA_PAD - LORA_WIDTH)))
    start_s = jnp.repeat(state_shift[0], DEC_SEQ, axis=0)
    start_all = jnp.stack([jnp.zeros((NS, 3 * rw + LORA_PAD), f32), pad_l(start_s)])
    mu_all = pad_l(mu_shift)
    rows_pad = lambda a, r0: jnp.pad(a, ((r0, LORA_PAD - r0 - a.shape[0]), (0, 0))).astype(bf16)
    scan_in = _rwkv_pre(proj, start_all, mu_all, w0, a0,
                        rows_pad(w_up[0], 0), rows_pad(a_up[0], DECAY_LORA), rows_pad(g_up[0], DECAY_LORA + ICLR_LORA))
    rk_flat = r_k.reshape(1, rw)
    zero_state = jnp.zeros((BATCH, N_HEADS, HEAD_DIM, HEAD_DIM), f32)
    ro_p, wkv_p = _scan(scan_in, k_k, k_a, rk_flat, lnx_w, lnx_b, zero_state, n_seq=BATCH, seq_len=SEQ, row0=0)
    ro_s, wkv_s = _scan(scan_in, k_k, k_a, rk_flat, lnx_w, lnx_b, jnp.swapaxes(state_wkv[0], -1, -2),
                        n_seq=DEC_BATCH, seq_len=DEC_SEQ, row0=NP)

    k_bf = proj[4].astype(bf16)
    vt_bf = jnp.swapaxes(proj[5, :NP].astype(bf16).reshape(NP // MOBA_BLOCK, MOBA_BLOCK, ATT_WIDTH), 1, 2)
    kmean = _kmean(proj, 4, NP // MOBA_BLOCK).reshape(BATCH, SEQ // MOBA_BLOCK, ATT_WIDTH)
    ao_p = _attn_prompt(rel_bias, proj, k_bf, vt_bf, kmean)
    rb_rows = jnp.repeat(rel_bias.T, DEC_SEQ, axis=0)
    pages_t = lambda c: jnp.transpose(c, (0, 1, 3, 4, 2))
    ao_s = _attn_sample(page_table, rb_rows, proj[3, NP:], proj[4, NP:], proj[5, NP:],
                        pages_t(cache_k), pages_t(cache_v))

    ro = jnp.concatenate([ro_p, ro_s], axis=0)
    ao = jnp.concatenate([ao_p, ao_s], axis=0)
    h1, xn2, gates, mask = _outproj(x_all, ro, ao, w_out[0].astype(bf16), norm2_g, router_w[0], router_b)

    tile_expert, tile_valid, sorted_tok, gate_sorted, dest4 = _route(gates, mask)
    x_sorted = jnp.take(xn2, sorted_tok, axis=0)
    act = _moe_up(tile_expert, tile_valid, x_sorted, w_gu[0], b_gu[0].reshape(N_EXPERTS, 1, 2 * D_FF))
    y_sorted = _moe_dn(tile_expert, tile_valid, act, w_dn[0], b_dn[0].reshape(N_EXPERTS, 1, D_MODEL), gate_sorted)
    y4 = jnp.take(y_sorted, dest4.T, axis=0)

    y = _final(h1, y4, pe, ple_norm_g, ple_gate_w[0].astype(bf16), ple_proj[0].astype(bf16),
               final_norm_g.reshape(1, D_MODEL))

    heads = lambda a, b, t: a.reshape(1, b, t, N_HEADS, HEAD_DIM)
    shift_rows = lambda r: jnp.concatenate([proj[0, r], proj[1, r], proj[2, r], proj[6, r, :LORA_WIDTH]], axis=-1)
    last_p = jnp.arange(BATCH) * SEQ + SEQ - 1
    last_s = NP + jnp.arange(DEC_BATCH) * DEC_SEQ + DEC_SEQ - 1
    return (y[:NP].reshape(BATCH, SEQ, D_MODEL),
            y[NP:].reshape(DEC_BATCH, DEC_SEQ, D_MODEL),
            heads(proj[4, :NP], BATCH, SEQ),
            heads(proj[5, :NP], BATCH, SEQ),
            jnp.swapaxes(wkv_p, -1, -2)[None],
            shift_rows(last_p)[None],
            heads(proj[4, NP:], DEC_BATCH, DEC_SEQ),
            heads(proj[5, NP:], DEC_BATCH, DEC_SEQ),
            jnp.swapaxes(wkv_s, -1, -2)[None],
            shift_rows(last_s)[None])
```

```python
import functools
import math

import jax
import jax.numpy as jnp
from jax import lax
from jax.experimental import pallas as pl
from jax.experimental.pallas import tpu as pltpu

f32 = jnp.float32
bf16 = jnp.bfloat16
i32 = jnp.int32
HI = lax.Precision.HIGHEST

D_MODEL = 2048
BATCH = 2
SEQ = 8192
DEC_BATCH = 32
DEC_SEQ = 8
PAST_LEN = 16384
PAGE_SIZE = 128
HEAD_DIM = 64
RWKV_WIDTH = 1024
ATT_WIDTH = 1024
N_HEADS = 16
DECAY_LORA = 64
ICLR_LORA = 64
GATE_LORA = 160
LORA_WIDTH = DECAY_LORA + ICLR_LORA + GATE_LORA
LORA_PAD = 384
GN_EPS = 64e-5
N_SHIFT = 3 * RWKV_WIDTH + LORA_WIDTH
MOBA_BLOCK = 256
MOBA_TOPK = 3
Q_BLOCK = 128
N_BUCKETS = 32
MAX_EXACT = 16
MAX_DISTANCE = 128
N_EXPERTS = 32
TOP_K = 4
D_FF = 2048
SWIGLU_LIMIT = 7.0
SWIGLU_ALPHA = 1.702
D_PLE = 256
RMS_EPS = 1e-6
NEG = -1e30

NP = BATCH * SEQ
NS = DEC_BATCH * DEC_SEQ
NTOK = NP + NS
N_PAGES = PAST_LEN // PAGE_SIZE
N_PAST_BLOCKS = PAST_LEN // MOBA_BLOCK
assert PAST_LEN % MOBA_BLOCK == 0

GROUP_W = 1024
SCAN_CHUNK = 64
SCAN_PREP_CHUNKS = 4
SCAN_PREP_CHAINS = 16
SCAN_APPLY_CHUNKS = 8
MOE_TM = 512
MOE_TILES = -(-(NTOK * TOP_K) // MOE_TM) + N_EXPERTS
MOE_ROWS = MOE_TILES * MOE_TM
PAGES_PER_STEP = 8
ATTN_UNROLL = 4
VMEM_LIMIT = 56 << 20


def _cp(sem, vmem=VMEM_LIMIT):
    return pltpu.CompilerParams(dimension_semantics=sem, vmem_limit_bytes=vmem)


def _dot(a, b, precision=None):
    return jnp.dot(a, b, preferred_element_type=f32, precision=precision)


def _dot_nt(a, b, precision=None):
    return lax.dot_general(a, b, (((1,), (1,)), ((), ())), preferred_element_type=f32, precision=precision)


def _dot_tn(a, b, precision=None):
    return lax.dot_general(a, b, (((0,), (0,)), ((), ())), preferred_element_type=f32, precision=precision)


def _rms(x, g):
    ms = jnp.mean(x * x, axis=-1, keepdims=True)
    return x * lax.rsqrt(ms + RMS_EPS) * g


def _topk_mask(vals, n_valid, k, axis=1):
    idx = lax.broadcasted_iota(i32, vals.shape, axis)
    idxf = idx.astype(f32)
    gm = jnp.where(idx < n_valid, vals, NEG)
    sel = jnp.zeros(vals.shape, f32)
    for _ in range(k):
        mx = jnp.max(gm, axis=axis, keepdims=True)
        cand = jnp.where(gm == mx, jnp.where(gm > 0.5 * NEG, idxf, 1e9), 1e9)
        first = jnp.min(cand, axis=axis, keepdims=True)
        pick = idxf == first
        sel = jnp.where(pick, 1.0, sel)
        gm = jnp.where(pick, NEG, gm)
    return sel


def _t5_bucket(d):
    n = jnp.maximum(d, 0)
    nf = jnp.maximum(n, 1).astype(f32)
    large = MAX_EXACT + (jnp.log(nf / MAX_EXACT) / math.log(MAX_DISTANCE / MAX_EXACT)
                         * (N_BUCKETS - MAX_EXACT)).astype(i32)
    large = jnp.minimum(large, N_BUCKETS - 1)
    return jnp.where(n < MAX_EXACT, n, large)


def _inproj_kernel(x_ref, g_ref, w_ref, o_ref, xn_ref):
    @pl.when(pl.program_id(1) == 0)
    def _():
        xn_ref[...] = _rms(x_ref[...], g_ref[...]).astype(bf16)

    o_ref[0] = _dot(xn_ref[...], w_ref[...])


def _inproj(x_all, g, w_big):
    n, d = x_all.shape
    tm = 640 if n % 640 == 0 else 256
    ng = w_big.shape[1] // GROUP_W
    return pl.pallas_call(
        _inproj_kernel,
        grid=(n // tm, ng),
        in_specs=[pl.BlockSpec((tm, d), lambda i, j: (i, 0)),
                  pl.BlockSpec((1, d), lambda i, j: (0, 0)),
                  pl.BlockSpec((d, GROUP_W), lambda i, j: (0, j))],
        out_specs=pl.BlockSpec((1, tm, GROUP_W), lambda i, j: (j, i, 0)),
        out_shape=jax.ShapeDtypeStruct((ng, n, GROUP_W), f32),
        scratch_shapes=[pltpu.VMEM((tm, d), bf16)],
        compiler_params=_cp(("arbitrary", "arbitrary")),
    )(x_all, g, w_big)


def _rwkv_pre_kernel(ur, uk, uv, ul, sr, sk, sv, sl, mur, muk, muv, mul, w0, a0, wup, aup, gup,
                     out_ref, carry, *, n_prompt_tiles):
    i = pl.program_id(0)
    tt = ur.shape[1]

    @pl.when(i == 0)
    def _():
        carry[...] = jnp.zeros_like(carry)

    rowi = lax.broadcasted_iota(i32, (tt, 1), 0)
    is_sample = i >= n_prompt_tiles
    pos = jnp.where(is_sample, rowi, i * tt + rowi)
    period_mask = jnp.where(is_sample, DEC_SEQ - 1, SEQ - 1)
    seq_start = (pos & period_mask) == 0

    def shift_mix(u_ref, s_ref, mu_ref, c0):
        u = u_ref[0]
        cw = u.shape[1]
        prev = jnp.where(rowi == 0, carry[0:1, c0:c0 + cw], pltpu.roll(u, 1, 0))
        sh = jnp.where(seq_start, s_ref[0], prev)
        carry[0:1, c0:c0 + cw] = u[tt - 1:tt, :]
        return u + (sh - u) * mu_ref[...]

    out_ref[0] = shift_mix(ur, sr, mur, 0)
    out_ref[1] = shift_mix(uk, sk, muk, RWKV_WIDTH)
    out_ref[2] = shift_mix(uv, sv, muv, 2 * RWKV_WIDTH)
    xl = shift_mix(ul, sl, mul, 3 * RWKV_WIDTH)
    lw = _dot(jnp.tanh(xl).astype(bf16), wup[...])
    la = _dot(xl.astype(bf16), aup[...])
    lg = _dot(jax.nn.sigmoid(xl).astype(bf16), gup[...])
    z = -(w0[...] + lw)
    softplus = jnp.maximum(z, 0.0) + jnp.log(1.0 + jnp.exp(-jnp.abs(z)))
    w_log = -softplus - 0.5
    out_ref[3] = -jnp.exp(w_log)
    out_ref[4] = jax.nn.sigmoid(a0[...] + la)
    out_ref[5] = lg


def _rwkv_pre(proj, start_all, mu_all, w0, a0, wup, aup, gup, tt=256):
    n = proj.shape[1]
    npt = NP // tt
    wcols = 3 * RWKV_WIDTH + LORA_PAD
    lblk = 3 * RWKV_WIDTH // LORA_PAD
    u_spec = lambda g: pl.BlockSpec((1, tt, GROUP_W), lambda i, g=g: (g, i, 0))
    s_spec = lambda g: pl.BlockSpec((1, tt, GROUP_W), lambda i, g=g: (jnp.where(i >= npt, 1, 0), 0, g))
    m_spec = lambda g: pl.BlockSpec((1, GROUP_W), lambda i, g=g: (0, g))
    full = lambda shp: pl.BlockSpec(shp, lambda i: tuple(0 for _ in shp))
    return pl.pallas_call(
        functools.partial(_rwkv_pre_kernel, n_prompt_tiles=npt),
        grid=(n // tt,),
        in_specs=[u_spec(0), u_spec(1), u_spec(2),
                  pl.BlockSpec((1, tt, LORA_PAD), lambda i: (6, i, 0)),
                  s_spec(0), s_spec(1), s_spec(2),
                  pl.BlockSpec((1, tt, LORA_PAD), lambda i: (jnp.where(i >= npt, 1, 0), 0, lblk)),
                  m_spec(0), m_spec(1), m_spec(2),
                  pl.BlockSpec((1, LORA_PAD), lambda i: (0, lblk)),
                  full((1, RWKV_WIDTH)), full((1, RWKV_WIDTH)),
                  full((LORA_PAD, RWKV_WIDTH)), full((LORA_PAD, RWKV_WIDTH)), full((LORA_PAD, RWKV_WIDTH))],
        out_specs=pl.BlockSpec((6, tt, GROUP_W), lambda i: (0, i, 0)),
        out_shape=jax.ShapeDtypeStruct((6, n, GROUP_W), f32),
        scratch_shapes=[pltpu.VMEM((8, wcols), f32)],
        compiler_params=_cp(("arbitrary",)),
    )(proj, proj, proj, proj, start_all, start_all, start_all, start_all,
      mu_all, mu_all, mu_all, mu_all, w0, a0, wup, aup, gup)


def _split(a):
    hi = a.astype(bf16)
    return hi, (a - hi.astype(f32)).astype(bf16)


def _mm(a, b, passes, dims=(((1,), (0,)), ((), ()))):
    dot = lambda x, y: lax.dot_general(x, y, dims, preferred_element_type=f32)
    if passes == 1:
        return dot(a.astype(bf16), b.astype(bf16))
    a_hi, a_lo = _split(a)
    b_hi, b_lo = _split(b)
    return dot(a_hi, b_hi) + (dot(a_lo, b_hi) + dot(a_hi, b_lo))


_NT = (((1,), (1,)), ((), ()))
_TN = (((0,), (0,)), ((), ()))


def _prefix_sum_rows(x):
    rows = x.shape[0]
    row = lax.broadcasted_iota(i32, x.shape, 0)
    s = 1
    while s < rows:
        x = x + jnp.where(row >= s, pltpu.roll(x, s, 0), 0.0)
        s *= 2
    return x


def _chunk_prepare(chains):
    c, n = chains[0][0].shape
    row = lax.broadcasted_iota(i32, (c, c), 0)
    col = lax.broadcasted_iota(i32, (c, c), 1)
    tri_incl = col <= row
    strict = col < row
    each = lambda fn, *lists: [fn(*xs) for xs in zip(*lists)]
    rs, ks, vs, kks, bs, lws = (list(x) for x in zip(*chains))
    cums = each(_prefix_sum_rows, lws)
    tots = [cum[c - 1:c, :] for cum in cums]
    p_invs = [jnp.exp(-cum) for cum in cums]
    decs = each(lambda tot, cum: jnp.exp(tot - cum), tots, cums)
    kk_ts = each(lambda kk, cum, lw: kk * jnp.exp(cum - lw), kks, cums, lws)
    r_ts = each(lambda r, cum: r * jnp.exp(cum), rs, cums)
    gs = each(lambda kk_t, r_t, k, b, p_inv: _mm(jnp.concatenate([kk_t, r_t], axis=0),
                                                 jnp.concatenate([k * p_inv, b * p_inv], axis=0), 1, _NT),
              kk_ts, r_ts, ks, bs, p_invs)
    l_bs = [jnp.where(strict, g[:c, c:], 0.0) for g in gs]
    t_invs = [(row == col).astype(f32) for _ in chains]
    s = 1
    while s < c:
        sh = s.bit_length() - 1
        off = (((row >> sh) & 1) == 1) & ((col >> sh) == (row >> sh) - 1)
        tmps = each(lambda l_b, t: _mm(jnp.where(off, l_b, 0.0), t, 1), l_bs, t_invs)
        t_invs = each(lambda t, tmp: t - _mm(t, tmp, 1), t_invs, tmps)
        s *= 2
    lvs = each(lambda g, v: _mm(jnp.concatenate([jnp.where(strict, g[:c, :c], 0.0),
                                                 jnp.where(tri_incl, g[c:, :c], 0.0)], axis=0), v, 1), gs, vs)
    ws = each(lambda t, kk_t, lv: _mm(t, jnp.concatenate([kk_t, lv[:c]], axis=1), 1), t_invs, kk_ts, lvs)
    aws = each(lambda g, w: _mm(jnp.where(tri_incl, g[c:, c:], 0.0), w, 1), gs, ws)
    xws = each(lambda b, dec, w: _mm(b * dec, w, 1, _TN), bs, decs, ws)
    kvs = each(lambda k, dec, v: _mm(k * dec, v, 1, _TN), ks, decs, vs)
    eye = lax.broadcasted_iota(i32, (n, n), 0) == lax.broadcasted_iota(i32, (n, n), 1)
    zs = each(lambda r_t, aw, tot, xw: jnp.concatenate(
        [r_t - aw[:, :n], jnp.where(eye, jnp.broadcast_to(jnp.exp(tot), (n, n)), 0.0) - xw[:, :n]], axis=0),
        r_ts, aws, tots, xws)
    ys = each(lambda lv, aw, kv, xw: jnp.concatenate([lv[c:] - aw[:, n:], kv - xw[:, n:]], axis=0),
              lvs, aws, kvs, xws)
    return zs, ys


def _scan_prep_kernel(in_ref, kk_ref, ka_ref, rk_ref, z_ref, y_ref, bonus_ref, *, tv, heads, cpp):
    n = HEAD_DIM

    def padded(x):
        if tv < SCAN_CHUNK:
            return jnp.concatenate([x, jnp.zeros((SCAN_CHUNK - tv, x.shape[1]), f32)], axis=0)
        return x

    chains = []
    for ci in range(cpp):
        rows = slice(ci * tv, (ci + 1) * tv)
        xr2, xk2, xv2, lw2, a2 = (padded(in_ref[g, rows, :]) for g in range(5))
        bonus = []
        for h in range(heads):
            sl = slice(h * n, (h + 1) * n)
            xr, xk, xv, lw, a = xr2[:, sl], xk2[:, sl], xv2[:, sl], lw2[:, sl], a2[:, sl]
            kkr = xk * kk_ref[:, sl]
            nrm = jnp.sqrt(jnp.sum(kkr * kkr, axis=-1, keepdims=True))
            kk = kkr / jnp.maximum(nrm, 1e-12)
            km = xk * (1.0 + (a - 1.0) * ka_ref[:, sl])
            chains.append((xr, km, xv, kk, kk * a, lw))
            bonus.append(jnp.sum(xr[:tv] * km[:tv] * rk_ref[:, sl], axis=-1, keepdims=True) * xv[:tv])
        bonus_ref[rows, :] = jnp.concatenate(bonus, axis=1)
    zs, ys = _chunk_prepare(chains)
    for ci in range(cpp):
        for p in range(heads // 2):
            i0 = ci * heads + 2 * p
            z_ref[ci, p] = jnp.concatenate(zs[i0:i0 + 2], axis=1)
            y_ref[ci, p] = jnp.concatenate(ys[i0:i0 + 2], axis=1)


def _scan_apply_kernel(z_ref, y_ref, bonus_ref, g_ref, lnw_ref, lnb_ref, s0_ref, o_ref, sout_ref, m_scr, *, tv, cps):
    c = pl.program_id(2)
    n = HEAD_DIM

    @pl.when(c == 0)
    def _():
        m_scr[...] = s0_ref[0]

    m = [m_scr[0], m_scr[1]]
    for ci in range(cps):
        z2, y2 = z_ref[ci, 0], y_ref[ci, 0]
        rows = slice(ci * tv, (ci + 1) * tv)
        outs = []
        for h in range(2):
            sl = slice(h * n, (h + 1) * n)
            res = _mm(z2[:, sl], m[h], 3) + y2[:, sl]
            m[h] = res[SCAN_CHUNK:]
            o = res[:tv]
            mean = jnp.mean(o, axis=-1, keepdims=True)
            var = jnp.mean(jnp.square(o - mean), axis=-1, keepdims=True)
            outs.append((o - mean) * lax.rsqrt(var + GN_EPS) * lnw_ref[:, sl] + lnb_ref[:, sl])
        o_ref[rows, :] = (jnp.concatenate(outs, axis=1) + bonus_ref[rows, :]) * g_ref[0, rows, :]
    m_scr[0] = m[0]
    m_scr[1] = m[1]

    @pl.when(c == pl.num_programs(2) - 1)
    def _():
        sout_ref[0] = m_scr[...]


def _scan(scan_in, k_k, k_a, r_k, lnx_w, lnx_b, state0_t, *, n_seq, seq_len, row0):
    tv = min(seq_len, SCAN_CHUNK)
    n_chunks = seq_len // tv
    cpp = min(SCAN_PREP_CHUNKS, n_chunks)
    heads = min(SCAN_PREP_CHAINS // cpp, N_HEADS)
    psteps = n_chunks // cpp
    prow = cpp * tv
    lanes = heads * HEAD_DIM
    n_rows = n_seq * seq_len
    pair = (2 * SCAN_CHUNK, 2 * HEAD_DIM)
    par4 = lambda: pl.BlockSpec((1, lanes), lambda s, p, c: (0, p))
    zy_shape = jax.ShapeDtypeStruct((n_seq * n_chunks, N_HEADS // 2) + pair, f32)
    zy_spec = pl.BlockSpec((cpp, heads // 2) + pair, lambda s, p, c: (s * psteps + c, p, 0, 0))
    z, y, bonus = pl.pallas_call(
        functools.partial(_scan_prep_kernel, tv=tv, heads=heads, cpp=cpp),
        grid=(n_seq, N_HEADS // heads, psteps),
        in_specs=[pl.BlockSpec((5, prow, lanes), lambda s, p, c: (0, row0 // prow + s * psteps + c, p)),
                  par4(), par4(), par4()],
        out_specs=[zy_spec, zy_spec, pl.BlockSpec((prow, lanes), lambda s, p, c: (s * psteps + c, p))],
        out_shape=[zy_shape, zy_shape, jax.ShapeDtypeStruct((n_rows, RWKV_WIDTH), f32)],
        compiler_params=_cp(("arbitrary", "arbitrary", "arbitrary")),
    )(scan_in, k_k, k_a, r_k)

    cps = min(SCAN_APPLY_CHUNKS, n_chunks)
    steps = n_chunks // cps
    rows = cps * tv
    par2 = lambda: pl.BlockSpec((1, 2 * HEAD_DIM), lambda s, p, c: (0, p))
    zy2 = pl.BlockSpec((cps, 1) + pair, lambda s, p, c: (s * steps + c, p, 0, 0))
    row_spec = pl.BlockSpec((rows, 2 * HEAD_DIM), lambda s, p, c: (s * steps + c, p))
    state_spec = pl.BlockSpec((1, 2, HEAD_DIM, HEAD_DIM), lambda s, p, c: (s, p, 0, 0))
    return pl.pallas_call(
        functools.partial(_scan_apply_kernel, tv=tv, cps=cps),
        grid=(n_seq, N_HEADS // 2, steps),
        in_specs=[zy2, zy2, row_spec,
                  pl.BlockSpec((1, rows, 2 * HEAD_DIM), lambda s, p, c: (5, row0 // rows + s * steps + c, p)),
                  par2(), par2(), state_spec],
        out_specs=[row_spec, state_spec],
        out_shape=[jax.ShapeDtypeStruct((n_rows, RWKV_WIDTH), f32),
                   jax.ShapeDtypeStruct((n_seq, N_HEADS, HEAD_DIM, HEAD_DIM), f32)],
        scratch_shapes=[pltpu.VMEM((2, HEAD_DIM, HEAD_DIM), f32)],
        compiler_params=_cp(("arbitrary", "arbitrary", "arbitrary")),
    )(z, y, bonus, scan_in, lnx_w, lnx_b, state0_t)


def _kmean_kernel(k_ref, o_ref):
    o_ref[0] = jnp.mean(k_ref[0], axis=0, keepdims=True)


def _kmean(proj, group, n_blocks):
    return pl.pallas_call(
        _kmean_kernel,
        grid=(n_blocks,),
        in_specs=[pl.BlockSpec((1, MOBA_BLOCK, GROUP_W), lambda i: (group, i, 0))],
        out_specs=pl.BlockSpec((1, 1, GROUP_W), lambda i: (i, 0, 0)),
        out_shape=jax.ShapeDtypeStruct((n_blocks, 1, GROUP_W), f32),
        compiler_params=_cp(("arbitrary",)),
    )(proj)


def _bias_tile(rb_ref, head, d):
    bucket = _t5_bucket(d)
    bias = jnp.zeros(d.shape, f32)
    for kb in range(N_BUCKETS):
        bias = jnp.where(bucket == kb, rb_ref[kb, head], bias)
    return jnp.where(d >= 0, bias - rb_ref[N_BUCKETS - 1, head], NEG)


def _fold8(x, op):
    parts = [x[i * 8:(i + 1) * 8] for i in range(x.shape[0] // 8)]
    while len(parts) > 1:
        parts = [op(parts[i], parts[i + 1]) for i in range(0, len(parts), 2)]
    return parts[0]


def _attn_prompt_kernel(rb_ref, q_ref, k_ref, vt_ref, km_ref, o_ref, bias_scr, s_scr, sel_scr):
    p = pl.program_id(1)
    own = pl.program_id(2)
    nq = MOBA_BLOCK
    n_blocks = km_ref.shape[1]
    unroll = ATTN_UNROLL
    own_slot = s_scr.shape[0] - 1
    prev_slot = own_slot - 1
    n_far = jnp.maximum(own - 1, 0)
    n_groups = (n_far + unroll - 1) // unroll

    @pl.when(own == 0)
    def _():
        di = lax.broadcasted_iota(i32, (MOBA_BLOCK, nq), 1) - lax.broadcasted_iota(i32, (MOBA_BLOCK, nq), 0)
        for variant in range(2):
            bias_scr[variant] = jnp.concatenate(
                [_bias_tile(rb_ref, 2 * p + h, di + variant * MOBA_BLOCK) for h in range(2)], axis=1)

    q_t = q_ref[0].T
    row_head = lax.broadcasted_iota(i32, q_t.shape, 0) >> 6
    scale = HEAD_DIM ** -0.5
    prev = jnp.maximum(own - 1, 0)
    q2 = jnp.concatenate([jnp.where(row_head == h, q_t, 0.0) for h in range(2)], axis=1)
    sel_scr[...] = _topk_mask(_dot(km_ref[0], q2, HI), own, MOBA_TOPK, axis=0)
    qs = (q2 * scale).astype(bf16)

    def keys(j):
        return k_ref[pl.ds(pl.multiple_of(j * MOBA_BLOCK, MOBA_BLOCK), MOBA_BLOCK), :]

    def chosen(j, limit):
        return jnp.where(j < limit, sel_scr[pl.ds(jnp.minimum(j, n_blocks - 1), 1), :], 0.0) > 0.5

    s_own = _dot(keys(own), qs) + bias_scr[0]
    s_prev = _dot(keys(prev), qs) + bias_scr[1]
    s_scr[own_slot] = s_own
    s_scr[prev_slot] = s_prev
    mrun = jnp.maximum(_fold8(s_own, jnp.maximum),
                       jnp.where(chosen(prev, own), _fold8(s_prev, jnp.maximum), NEG))

    def pass1(g, mrun):
        for u in range(unroll):
            j = g * unroll + u
            s = _dot(keys(jnp.minimum(j, n_blocks - 1)), qs)
            s_scr[j] = s
            mrun = jnp.maximum(mrun, jnp.where(chosen(j, n_far), _fold8(s, jnp.maximum), NEG))
        return mrun

    mrun = lax.fori_loop(0, n_groups, pass1, mrun)
    m = jnp.max(mrun, axis=0, keepdims=True)

    def weighted(slot, j, keep):
        pj = jnp.exp(s_scr[slot] - jnp.where(keep, m, -NEG))
        return _fold8(pj, jnp.add), _dot(vt_ref[j], pj.astype(bf16))

    l_own, a_own = weighted(own_slot, own, True)
    l_prev, a_prev = weighted(prev_slot, prev, chosen(prev, own))

    def pass2(g, carry):
        lrun, acc = carry
        for u in range(unroll):
            j = g * unroll + u
            lj, aj = weighted(j, jnp.minimum(j, n_blocks - 1), chosen(j, n_far))
            lrun = lrun + lj
            acc = acc + aj
        return lrun, acc

    lrun, acc = lax.fori_loop(0, n_groups, pass2, (l_own + l_prev, a_own + a_prev))
    out2 = acc / jnp.sum(lrun, axis=0, keepdims=True)
    o_ref[...] = jnp.where(row_head == 0, out2[:, :nq], out2[:, nq:]).T


def _attn_prompt(rel_bias, proj, k_bf, vt_bf, kmean, q_group=3):
    n_blocks = SEQ // MOBA_BLOCK
    lanes = 2 * HEAD_DIM
    cols = 2 * MOBA_BLOCK
    return pl.pallas_call(
        _attn_prompt_kernel,
        grid=(BATCH, N_HEADS // 2, n_blocks),
        in_specs=[pl.BlockSpec(memory_space=pltpu.SMEM),
                  pl.BlockSpec((1, MOBA_BLOCK, lanes), lambda b, p, qi: (q_group, b * n_blocks + qi, p)),
                  pl.BlockSpec((SEQ, lanes), lambda b, p, qi: (b, p)),
                  pl.BlockSpec((n_blocks, lanes, MOBA_BLOCK), lambda b, p, qi: (b, p, 0)),
                  pl.BlockSpec((1, n_blocks, lanes), lambda b, p, qi: (b, 0, p))],
        out_specs=pl.BlockSpec((MOBA_BLOCK, lanes), lambda b, p, qi: (b * n_blocks + qi, p)),
        out_shape=jax.ShapeDtypeStruct((NP, ATT_WIDTH), f32),
        scratch_shapes=[pltpu.VMEM((2, MOBA_BLOCK, cols), f32),
                        pltpu.VMEM((n_blocks + ATTN_UNROLL + 2, MOBA_BLOCK, cols), f32),
                        pltpu.VMEM((n_blocks, cols), f32)],
        compiler_params=_cp(("arbitrary", "arbitrary", "arbitrary")),
    )(rel_bias, proj, k_bf, vt_bf, kmean)


def _attn_sample_kernel(pt_ref, rb_ref, q_ref, kn_ref, vn_ref, *rest):
    pps = PAGES_PER_STEP
    kp, vp = rest[:pps], rest[pps:2 * pps]
    o_ref = rest[2 * pps]
    s_scr, gate_scr, qm_scr, qc_scr, acc_scr, l_scr = rest[2 * pps + 1:]
    step = pl.program_id(1)
    hd = HEAD_DIM
    head_rows = lambda h: slice(h * DEC_SEQ, (h + 1) * DEC_SEQ)
    n_steps = N_PAGES // pps
    pages_per_block = MOBA_BLOCK // PAGE_SIZE
    rows = N_HEADS * DEC_SEQ
    row = lax.broadcasted_iota(i32, (rows, 1), 0)
    trow = row & (DEC_SEQ - 1)
    hrow = row >> 3
    blk = lax.broadcasted_iota(i32, (rows, N_PAST_BLOCKS), 1)

    @pl.when(step == 0)
    def _():
        qt = jnp.concatenate([q_ref[...]] * N_HEADS, axis=0)
        lane_head = lax.broadcasted_iota(i32, qt.shape, 1) >> 6
        qm_scr[...] = jnp.where(lane_head == hrow, qt * (HEAD_DIM ** -0.5), 0.0).astype(bf16)
        q = q_ref[...] * (HEAD_DIM ** -0.5)
        for h in range(N_HEADS):
            qc_scr[head_rows(h), :] = q[:, h * hd:(h + 1) * hd]
        gate_scr[...] = jnp.zeros_like(gate_scr)

    @pl.when(step < n_steps)
    def _():
        g = gate_scr[...]
        qh = [qc_scr[head_rows(h), :].astype(bf16) for h in range(N_HEADS)]
        for i in range(pps):
            page = step * pps + i
            sc = jnp.concatenate([_dot(qh[h], kp[i][0, 0, h].astype(bf16)) for h in range(N_HEADS)],
                                 axis=0)
            s_scr[page] = sc
            g = g + jnp.where(blk == page // pages_per_block, jnp.sum(sc, axis=1, keepdims=True), 0.0)
        gate_scr[...] = g

    @pl.when(step == n_steps - 1)
    def _():
        sel = _topk_mask(gate_scr[...], N_PAST_BLOCKS, MOBA_TOPK)
        d_own = trow - lax.broadcasted_iota(i32, (rows, DEC_SEQ), 1)
        s_own = _dot_nt(qm_scr[...], kn_ref[...].astype(bf16))
        b_own = jnp.zeros(s_own.shape, f32)
        for kb in range(DEC_SEQ):
            b_own = jnp.where(d_own == kb, rb_ref[:, kb:kb + 1], b_own)
        s_own = jnp.where(d_own >= 0, s_own + b_own, NEG)
        far_bias = rb_ref[:, N_BUCKETS - 1:N_BUCKETS]
        def masked(page, bias):
            chosen = jnp.sum(jnp.where(blk == page // pages_per_block, sel, 0.0), axis=1, keepdims=True) > 0.5
            sj = jnp.where(chosen, s_scr[page] + bias, NEG)
            s_scr[page] = sj
            return jnp.max(sj, axis=1, keepdims=True)

        n_far = N_PAGES - pages_per_block
        m = jnp.max(s_own, axis=1, keepdims=True)
        m = lax.fori_loop(0, n_far, lambda j, m: jnp.maximum(m, masked(j, far_bias)), m)
        for page in range(n_far, N_PAGES):
            d_near = (PAST_LEN - page * PAGE_SIZE) + trow - lax.broadcasted_iota(i32, (rows, PAGE_SIZE), 1)
            bucket = _t5_bucket(d_near)
            b_near = jnp.zeros(d_near.shape, f32)
            for kb in range(N_BUCKETS):
                b_near = jnp.where(bucket == kb, rb_ref[:, kb:kb + 1], b_near)
            m = jnp.maximum(m, masked(page, b_near))

        def expo(j, l):
            pj = jnp.exp(s_scr[j] - m)
            s_scr[j] = pj
            return l + jnp.sum(pj, axis=1, keepdims=True)

        p_own = jnp.exp(s_own - m)
        l_scr[...] = lax.fori_loop(0, N_PAGES, expo, jnp.sum(p_own, axis=1, keepdims=True))
        acc = jnp.zeros((rows, ATT_WIDTH), f32)
        vn = vn_ref[...]
        for t in range(DEC_SEQ):
            acc = acc + p_own[:, t:t + 1] * vn[t:t + 1, :]
        for h in range(N_HEADS):
            acc_scr[head_rows(h), :] = acc[head_rows(h), h * hd:(h + 1) * hd]

    @pl.when(step >= n_steps)
    def _():
        acc = acc_scr[...]
        for i in range(pps):
            page = (step - n_steps) * pps + i
            acc = acc + jnp.concatenate([_dot_nt(s_scr[page, head_rows(h), :].astype(bf16),
                                                 vp[i][0, 0, h].astype(bf16))
                                         for h in range(N_HEADS)], axis=0)
        acc_scr[...] = acc

    @pl.when(step == 2 * n_steps - 1)
    def _():
        a = acc_scr[...] / l_scr[...]
        o_ref[...] = jnp.concatenate([a[head_rows(h), :] for h in range(N_HEADS)], axis=1)


def _attn_sample(page_table, rb_rows, q_s, k_new, v_new, pool_k, pool_v):
    pps = PAGES_PER_STEP
    n_steps = N_PAGES // pps
    rows = N_HEADS * DEC_SEQ
    tok = pl.BlockSpec((DEC_SEQ, ATT_WIDTH), lambda b, s, pt: (b, 0))
    page_block = (1, 1, N_HEADS, HEAD_DIM, PAGE_SIZE)

    def kspec(i):
        return pl.BlockSpec(page_block, lambda b, s, pt, i=i:
                            (0, pt[b * N_PAGES + jnp.minimum(s, n_steps - 1) * pps + i], 0, 0, 0))

    def vspec(i):
        return pl.BlockSpec(page_block, lambda b, s, pt, i=i:
                            (0, pt[b * N_PAGES + jnp.maximum(s - n_steps, 0) * pps + i], 0, 0, 0))

    grid_spec = pltpu.PrefetchScalarGridSpec(
        num_scalar_prefetch=1,
        grid=(DEC_BATCH, 2 * n_steps),
        in_specs=[pl.BlockSpec((rows, N_BUCKETS), lambda b, s, pt: (0, 0)), tok, tok, tok]
                 + [kspec(i) for i in range(pps)] + [vspec(i) for i in range(pps)],
        out_specs=pl.BlockSpec((DEC_SEQ, ATT_WIDTH), lambda b, s, pt: (b, 0)),
        scratch_shapes=[pltpu.VMEM((N_PAGES, rows, PAGE_SIZE), f32),
                        pltpu.VMEM((rows, N_PAST_BLOCKS), f32),
                        pltpu.VMEM((rows, ATT_WIDTH), bf16),
                        pltpu.VMEM((rows, HEAD_DIM), f32),
                        pltpu.VMEM((rows, HEAD_DIM), f32),
                        pltpu.VMEM((rows, 1), f32)])
    return pl.pallas_call(
        _attn_sample_kernel,
        grid_spec=grid_spec,
        out_shape=jax.ShapeDtypeStruct((NS, ATT_WIDTH), f32),
        compiler_params=_cp(("arbitrary", "arbitrary")),
    )(page_table.reshape(-1), rb_rows, q_s, k_new, v_new, *([pool_k] * pps), *([pool_v] * pps))


def _outproj_kernel(x_ref, ro_ref, ao_ref, wo_ref, g2_ref, rw_ref, rb_ref, h_ref, xn_ref, gate_ref, mask_ref):
    h = (x_ref[...] + _dot(ro_ref[...].astype(bf16), wo_ref[:RWKV_WIDTH, :])
         + _dot(ao_ref[...].astype(bf16), wo_ref[RWKV_WIDTH:, :]))
    h_ref[...] = h
    xn = _rms(h, g2_ref[...])
    xn_ref[...] = xn.astype(bf16)
    logits = _dot(xn, rw_ref[...], HI) + rb_ref[...]
    sel = _topk_mask(logits, N_EXPERTS, TOP_K)
    mx = jnp.max(jnp.where(sel > 0.5, logits, NEG), axis=1, keepdims=True)
    e = jnp.where(sel > 0.5, jnp.exp(logits - mx), 0.0)
    gate_ref[...] = e / jnp.sum(e, axis=1, keepdims=True)
    mask_ref[...] = sel


def _outproj(x_all, ro, ao, wo_bf, g2, router_w, router_b, tm=256):
    n, d = x_all.shape
    row = lambda w: pl.BlockSpec((tm, w), lambda i: (i, 0))
    full = lambda shp: pl.BlockSpec(shp, lambda i: tuple(0 for _ in shp))
    return pl.pallas_call(
        _outproj_kernel,
        grid=(n // tm,),
        in_specs=[row(d), row(RWKV_WIDTH), row(ATT_WIDTH), full((d, d)), full((1, d)),
                  full((d, N_EXPERTS)), full((1, N_EXPERTS))],
        out_specs=[row(d), row(d), row(N_EXPERTS), row(N_EXPERTS)],
        out_shape=[jax.ShapeDtypeStruct((n, d), f32), jax.ShapeDtypeStruct((n, d), bf16),
                   jax.ShapeDtypeStruct((n, N_EXPERTS), f32), jax.ShapeDtypeStruct((n, N_EXPERTS), f32)],
        compiler_params=_cp(("arbitrary",)),
    )(x_all, ro, ao, wo_bf, g2, router_w, router_b)


def _expert_changed(te_ref, t):
    return jnp.logical_or(t == 0, te_ref[t] != te_ref[jnp.maximum(t - 1, 0)])


def _moe_up_kernel(te_ref, tv_ref, x_ref, wg_ref, wl_ref, bg_ref, bl_ref, act_ref, wg_s, wl_s):
    t = pl.program_id(1)

    @pl.when(_expert_changed(te_ref, t))
    def _():
        wg_s[...] = wg_ref[0].astype(bf16)
        wl_s[...] = wl_ref[0].astype(bf16)

    @pl.when(tv_ref[t] == 1)
    def _():
        x = x_ref[...]
        glu = jnp.minimum(_dot(x, wg_s[...]) + bg_ref[0], SWIGLU_LIMIT)
        lin = jnp.clip(_dot(x, wl_s[...]) + bl_ref[0], -SWIGLU_LIMIT, SWIGLU_LIMIT)
        act_ref[...] = (glu * jax.nn.sigmoid(SWIGLU_ALPHA * glu) * (lin + 1.0)).astype(bf16)

    @pl.when(tv_ref[t] == 0)
    def _():
        act_ref[...] = jnp.zeros_like(act_ref)


def _moe_up(tile_expert, tile_valid, x_sorted, w_gu, b_gu3, tf=512):
    nj = D_FF // tf
    grid_spec = pltpu.PrefetchScalarGridSpec(
        num_scalar_prefetch=2,
        grid=(nj, MOE_TILES),
        in_specs=[pl.BlockSpec((MOE_TM, D_MODEL), lambda j, t, te, tv: (t, 0)),
                  pl.BlockSpec((1, D_MODEL, tf), lambda j, t, te, tv: (te[t], 0, j)),
                  pl.BlockSpec((1, D_MODEL, tf), lambda j, t, te, tv: (te[t], 0, nj + j)),
                  pl.BlockSpec((1, 1, tf), lambda j, t, te, tv: (te[t], 0, j)),
                  pl.BlockSpec((1, 1, tf), lambda j, t, te, tv: (te[t], 0, nj + j))],
        out_specs=pl.BlockSpec((MOE_TM, tf), lambda j, t, te, tv: (t, j)),
        scratch_shapes=[pltpu.VMEM((D_MODEL, tf), bf16), pltpu.VMEM((D_MODEL, tf), bf16)])
    return pl.pallas_call(
        _moe_up_kernel,
        grid_spec=grid_spec,
        out_shape=jax.ShapeDtypeStruct((MOE_ROWS, D_FF), bf16),
        compiler_params=_cp(("arbitrary", "arbitrary")),
    )(tile_expert, tile_valid, x_sorted, w_gu, w_gu, b_gu3, b_gu3)


def _moe_dn_kernel(te_ref, tv_ref, act_ref, wd_ref, bd_ref, gs_ref, y_ref, wd_s):
    t = pl.program_id(1)

    @pl.when(_expert_changed(te_ref, t))
    def _():
        wd_s[...] = wd_ref[0].astype(bf16)

    @pl.when(tv_ref[t] == 1)
    def _():
        y_ref[...] = (_dot(act_ref[...], wd_s[...]) + bd_ref[0]) * gs_ref[...]

    @pl.when(tv_ref[t] == 0)
    def _():
        y_ref[...] = jnp.zeros_like(y_ref)


def _moe_dn(tile_expert, tile_valid, act, w_dn, b_dn3, gate_sorted, tn=1024):
    nj = D_MODEL // tn
    grid_spec = pltpu.PrefetchScalarGridSpec(
        num_scalar_prefetch=2,
        grid=(nj, MOE_TILES),
        in_specs=[pl.BlockSpec((MOE_TM, D_FF), lambda j, t, te, tv: (t, 0)),
                  pl.BlockSpec((1, D_FF, tn), lambda j, t, te, tv: (te[t], 0, j)),
                  pl.BlockSpec((1, 1, tn), lambda j, t, te, tv: (te[t], 0, j)),
                  pl.BlockSpec((MOE_TM, 1), lambda j, t, te, tv: (t, 0))],
        out_specs=pl.BlockSpec((MOE_TM, tn), lambda j, t, te, tv: (t, j)),
        scratch_shapes=[pltpu.VMEM((D_FF, tn), bf16)])
    return pl.pallas_call(
        _moe_dn_kernel,
        grid_spec=grid_spec,
        out_shape=jax.ShapeDtypeStruct((MOE_ROWS, D_MODEL), f32),
        compiler_params=_cp(("arbitrary", "arbitrary")),
    )(tile_expert, tile_valid, act, w_dn, b_dn3, gate_sorted)


def _route(gates, mask):
    n = gates.shape[0]
    mi = mask.astype(i32)
    counts = jnp.sum(mi, axis=0)
    rank = jnp.cumsum(mi, axis=0) - mi
    tiles_per = (counts + MOE_TM - 1) // MOE_TM
    tiles_end = jnp.cumsum(tiles_per)
    row_start = (tiles_end - tiles_per) * MOE_TM
    dest = row_start[None, :] + rank
    used = tiles_end[-1]
    t = jnp.arange(MOE_TILES, dtype=i32)
    te = jnp.sum((tiles_end[None, :] <= t[:, None]).astype(i32), axis=1)
    last_e = jnp.max(jnp.where(tiles_per > 0, jnp.arange(N_EXPERTS, dtype=i32), 0))
    tile_valid = (t < used).astype(i32)
    tile_expert = jnp.where(t < used, jnp.minimum(te, N_EXPERTS - 1), last_e)
    _, idx4 = lax.top_k(mask, TOP_K)
    in_bounds = dict(mode="promise_in_bounds")
    dest4 = jnp.take_along_axis(dest, idx4, axis=1, **in_bounds)
    g4 = jnp.take_along_axis(gates, idx4, axis=1, **in_bounds)
    order = jnp.argsort(idx4.reshape(-1), stable=True).astype(i32)
    per_row = lambda per_expert: jnp.repeat(per_expert[tile_expert], MOE_TM)
    off = jnp.arange(MOE_ROWS, dtype=i32) - per_row(row_start)
    live = (off < per_row(counts)) & (jnp.repeat(tile_valid, MOE_TM) == 1)
    src = order[jnp.clip(off + per_row(jnp.cumsum(counts) - counts), 0, n * TOP_K - 1)]
    sorted_tok = jnp.where(live, src // TOP_K, 0)
    gate_sorted = jnp.where(live, g4.reshape(-1)[src], 0.0)
    return tile_expert, tile_valid, sorted_tok, gate_sorted.reshape(-1, 1), dest4


def _final_kernel(h_ref, y4_ref, pe_ref, gn_ref, gw_ref, pp_ref, fn_ref, o_ref):
    h = h_ref[...]
    y = y4_ref[0]
    for kk in range(1, TOP_K):
        y = y + y4_ref[kk]
    h = h + y
    gate = jax.nn.sigmoid(_dot(_rms(h, gn_ref[...]).astype(bf16), gw_ref[...]))
    h = h + gate * _dot(pe_ref[...].astype(bf16), pp_ref[...])
    o_ref[...] = _rms(h, fn_ref[...])


def _final(h1, y4, pe, ple_norm_g, gw_bf, pp_bf, final_norm_g, tm=128):
    n, d = h1.shape
    full = lambda shp: pl.BlockSpec(shp, lambda i: tuple(0 for _ in shp))
    return pl.pallas_call(
        _final_kernel,
        grid=(n // tm,),
        in_specs=[pl.BlockSpec((tm, d), lambda i: (i, 0)),
                  pl.BlockSpec((TOP_K, tm, d), lambda i: (0, i, 0)),
                  pl.BlockSpec((tm, D_PLE), lambda i: (i, 0)),
                  full((1, d)), full((d, d)), full((D_PLE, d)), full((1, d))],
        out_specs=pl.BlockSpec((tm, d), lambda i: (i, 0)),
        out_shape=jax.ShapeDtypeStruct((n, d), f32),
        compiler_params=_cp(("arbitrary",)),
    )(h1, y4, pe, ple_norm_g, gw_bf, pp_bf, final_norm_g)


def kernel(x_prompt, x_sample, cache_k, cache_v, state_wkv, state_shift, page_table, p_prompt, p_sample, norm1_g, w_in, mu_shift, w0, w_up, a0, a_up, g_up, k_k, k_a, r_k, lnx_w, lnx_b, w_out, rel_bias, norm2_g, router_w, router_b, w_gu, b_gu, w_dn, b_dn, ple_norm_g, ple_gate_w, ple_proj, final_norm_g):
    rw = RWKV_WIDTH
    x_all = jnp.concatenate([x_prompt.reshape(NP, D_MODEL), x_sample.reshape(NS, D_MODEL)], axis=0)
    pe = jnp.concatenate([p_prompt[0].reshape(NP, D_PLE), p_sample[0].reshape(NS, D_PLE)], axis=0)

    w = w_in[0]
    w_big = jnp.concatenate([w[:, :3 * rw], w[:, N_SHIFT:],
                             jnp.pad(w[:, 3 * rw:N_SHIFT], ((0, 0), (0, GROUP_W - LORA_WIDTH)))], axis=1).astype(bf16)
    proj = _inproj(x_all, norm1_g, w_big)

    pad_l = lambda a: jnp.pad(a, ((0, 0), (0, LORA_PAD - LORA_WIDTH)))
    start_s = jnp.repeat(state_shift[0], DEC_SEQ, axis=0)
    start_all = jnp.stack([jnp.zeros((NS, 3 * rw + LORA_PAD), f32), pad_l(start_s)])
    mu_all = pad_l(mu_shift)
    rows_pad = lambda a, r0: jnp.pad(a, ((r0, LORA_PAD - r0 - a.shape[0]), (0, 0))).astype(bf16)
    scan_in = _rwkv_pre(proj, start_all, mu_all, w0, a0,
                        rows_pad(w_up[0], 0), rows_pad(a_up[0], DECAY_LORA), rows_pad(g_up[0], DECAY_LORA + ICLR_LORA))
    rk_flat = r_k.reshape(1, rw)
    zero_state = jnp.zeros((BATCH, N_HEADS, HEAD_DIM, HEAD_DIM), f32)
    ro_p, wkv_p = _scan(scan_in, k_k, k_a, rk_flat, lnx_w, lnx_b, zero_state, n_seq=BATCH, seq_len=SEQ, row0=0)
    ro_s, wkv_s = _scan(scan_in, k_k, k_a, rk_flat, lnx_w, lnx_b, jnp.swapaxes(state_wkv[0], -1, -2),
                        n_seq=DEC_BATCH, seq_len=DEC_SEQ, row0=NP)

    k_bf = proj[4].astype(bf16)
    vt_bf = jnp.swapaxes(proj[5, :NP].astype(bf16).reshape(NP // MOBA_BLOCK, MOBA_BLOCK, ATT_WIDTH), 1, 2)
    kmean = _kmean(proj, 4, NP // MOBA_BLOCK).reshape(BATCH, SEQ // MOBA_BLOCK, ATT_WIDTH)
    ao_p = _attn_prompt(rel_bias, proj, k_bf, vt_bf, kmean)
    rb_rows = jnp.repeat(rel_bias.T, DEC_SEQ, axis=0)
    pages_t = lambda c: jnp.transpose(c, (0, 1, 3, 4, 2))
    ao_s = _attn_sample(page_table, rb_rows, proj[3, NP:], proj[4, NP:], proj[5, NP:],
                        pages_t(cache_k), pages_t(cache_v))

    ro = jnp.concatenate([ro_p, ro_s], axis=0)
    ao = jnp.concatenate([ao_p, ao_s], axis=0)
    h1, xn2, gates, mask = _outproj(x_all, ro, ao, w_out[0].astype(bf16), norm2_g, router_w[0], router_b)

    tile_expert, tile_valid, sorted_tok, gate_sorted, dest4 = _route(gates, mask)
    x_sorted = xn2.at[sorted_tok].get(mode="promise_in_bounds")
    act = _moe_up(tile_expert, tile_valid, x_sorted, w_gu[0], b_gu[0].reshape(N_EXPERTS, 1, 2 * D_FF))
    y_sorted = _moe_dn(tile_expert, tile_valid, act, w_dn[0], b_dn[0].reshape(N_EXPERTS, 1, D_MODEL), gate_sorted)
    y4 = y_sorted.at[dest4.T].get(mode="promise_in_bounds")

    y = _final(h1, y4, pe, ple_norm_g, ple_gate_w[0].astype(bf16), ple_proj[0].astype(bf16),
               final_norm_g.reshape(1, D_MODEL))

    heads = lambda a, b, t: a.reshape(1, b, t, N_HEADS, HEAD_DIM)
    def shift_rows(row0, n_seq, seq_len):
        last = lambda g, w: proj[g, row0 + seq_len - 1:row0 + n_seq * seq_len:seq_len, :w]
        return jnp.concatenate([last(0, rw), last(1, rw), last(2, rw), last(6, LORA_WIDTH)], axis=-1)

    last_p = (0, BATCH, SEQ)
    last_s = (NP, DEC_BATCH, DEC_SEQ)
    return (y[:NP].reshape(BATCH, SEQ, D_MODEL),
            y[NP:].reshape(DEC_BATCH, DEC_SEQ, D_MODEL),
            heads(proj[4, :NP], BATCH, SEQ),
            heads(proj[5, :NP], BATCH, SEQ),
            jnp.swapaxes(wkv_p, -1, -2)[None],
            shift_rows(*last_p)[None],
            heads(proj[4, NP:], DEC_BATCH, DEC_SEQ),
            heads(proj[5, NP:], DEC_BATCH, DEC_SEQ),
            jnp.swapaxes(wkv_s, -1, -2)[None],
            shift_rows(*last_s)[None])
```

```python
import functools
import math

import jax
import jax.numpy as jnp
from jax import lax
from jax.experimental import pallas as pl
from jax.experimental.pallas import tpu as pltpu

f32 = jnp.float32
bf16 = jnp.bfloat16
i32 = jnp.int32
HI = lax.Precision.HIGHEST

D_MODEL = 2048
BATCH = 2
SEQ = 8192
DEC_BATCH = 32
DEC_SEQ = 8
PAST_LEN = 16384
PAGE_SIZE = 128
HEAD_DIM = 64
RWKV_WIDTH = 1024
ATT_WIDTH = 1024
N_HEADS = 16
DECAY_LORA = 64
ICLR_LORA = 64
GATE_LORA = 160
LORA_WIDTH = DECAY_LORA + ICLR_LORA + GATE_LORA
LORA_PAD = 384
GN_EPS = 64e-5
N_SHIFT = 3 * RWKV_WIDTH + LORA_WIDTH
MOBA_BLOCK = 256
MOBA_TOPK = 3
Q_BLOCK = 128
N_BUCKETS = 32
MAX_EXACT = 16
MAX_DISTANCE = 128
N_EXPERTS = 32
TOP_K = 4
D_FF = 2048
SWIGLU_LIMIT = 7.0
SWIGLU_ALPHA = 1.702
D_PLE = 256
RMS_EPS = 1e-6
NEG = -1e30

NP = BATCH * SEQ
NS = DEC_BATCH * DEC_SEQ
NTOK = NP + NS
N_PAGES = PAST_LEN // PAGE_SIZE
N_PAST_BLOCKS = PAST_LEN // MOBA_BLOCK
assert PAST_LEN % MOBA_BLOCK == 0

GROUP_W = 1024
SCAN_CHUNK = 64
SCAN_PREP_CHUNKS = 4
SCAN_PREP_CHAINS = 16
SCAN_APPLY_CHUNKS = 8
MOE_TM = 512
MOE_TILES = -(-(NTOK * TOP_K) // MOE_TM) + N_EXPERTS
MOE_ROWS = MOE_TILES * MOE_TM
PAGES_PER_STEP = 16
ATTN_UNROLL = 4
VMEM_LIMIT = 56 << 20


def _cp(sem, vmem=VMEM_LIMIT):
    return pltpu.CompilerParams(dimension_semantics=sem, vmem_limit_bytes=vmem)


def _dot(a, b, precision=None):
    return jnp.dot(a, b, preferred_element_type=f32, precision=precision)


def _dot_nt(a, b, precision=None):
    return lax.dot_general(a, b, (((1,), (1,)), ((), ())), preferred_element_type=f32, precision=precision)


def _dot_tn(a, b, precision=None):
    return lax.dot_general(a, b, (((0,), (0,)), ((), ())), preferred_element_type=f32, precision=precision)


def _rms(x, g):
    ms = jnp.mean(x * x, axis=-1, keepdims=True)
    return x * lax.rsqrt(ms + RMS_EPS) * g


def _topk_mask(vals, n_valid, k, axis=1):
    idx = lax.broadcasted_iota(i32, vals.shape, axis)
    idxf = idx.astype(f32)
    gm = jnp.where(idx < n_valid, vals, NEG)
    sel = jnp.zeros(vals.shape, f32)
    for _ in range(k):
        mx = jnp.max(gm, axis=axis, keepdims=True)
        cand = jnp.where(gm == mx, jnp.where(gm > 0.5 * NEG, idxf, 1e9), 1e9)
        first = jnp.min(cand, axis=axis, keepdims=True)
        pick = idxf == first
        sel = jnp.where(pick, 1.0, sel)
        gm = jnp.where(pick, NEG, gm)
    return sel


def _t5_bucket(d):
    n = jnp.maximum(d, 0)
    nf = jnp.maximum(n, 1).astype(f32)
    large = MAX_EXACT + (jnp.log(nf / MAX_EXACT) / math.log(MAX_DISTANCE / MAX_EXACT)
                         * (N_BUCKETS - MAX_EXACT)).astype(i32)
    large = jnp.minimum(large, N_BUCKETS - 1)
    return jnp.where(n < MAX_EXACT, n, large)


def _inproj_kernel(x_ref, g_ref, w_ref, o_ref, xn_ref):
    @pl.when(pl.program_id(1) == 0)
    def _():
        xn_ref[...] = _rms(x_ref[...], g_ref[...]).astype(bf16)

    o_ref[0] = _dot(xn_ref[...], w_ref[...])


def _inproj(x_all, g, w_big):
    n, d = x_all.shape
    tm = 640 if n % 640 == 0 else 256
    ng = w_big.shape[1] // GROUP_W
    return pl.pallas_call(
        _inproj_kernel,
        grid=(n // tm, ng),
        in_specs=[pl.BlockSpec((tm, d), lambda i, j: (i, 0)),
                  pl.BlockSpec((1, d), lambda i, j: (0, 0)),
                  pl.BlockSpec((d, GROUP_W), lambda i, j: (0, j))],
        out_specs=pl.BlockSpec((1, tm, GROUP_W), lambda i, j: (j, i, 0)),
        out_shape=jax.ShapeDtypeStruct((ng, n, GROUP_W), f32),
        scratch_shapes=[pltpu.VMEM((tm, d), bf16)],
        compiler_params=_cp(("arbitrary", "arbitrary")),
    )(x_all, g, w_big)


def _rwkv_pre_kernel(ur, uk, uv, ul, sr, sk, sv, sl, mur, muk, muv, mul, w0, a0, wup, aup, gup,
                     out_ref, carry, *, n_prompt_tiles):
    i = pl.program_id(0)
    tt = ur.shape[1]

    @pl.when(i == 0)
    def _():
        carry[...] = jnp.zeros_like(carry)

    rowi = lax.broadcasted_iota(i32, (tt, 1), 0)
    is_sample = i >= n_prompt_tiles
    pos = jnp.where(is_sample, rowi, i * tt + rowi)
    period_mask = jnp.where(is_sample, DEC_SEQ - 1, SEQ - 1)
    seq_start = (pos & period_mask) == 0

    def shift_mix(u_ref, s_ref, mu_ref, c0):
        u = u_ref[0]
        cw = u.shape[1]
        prev = jnp.where(rowi == 0, carry[0:1, c0:c0 + cw], pltpu.roll(u, 1, 0))
        sh = jnp.where(seq_start, s_ref[0], prev)
        carry[0:1, c0:c0 + cw] = u[tt - 1:tt, :]
        return u + (sh - u) * mu_ref[...]

    out_ref[0] = shift_mix(ur, sr, mur, 0)
    out_ref[1] = shift_mix(uk, sk, muk, RWKV_WIDTH)
    out_ref[2] = shift_mix(uv, sv, muv, 2 * RWKV_WIDTH)
    xl = shift_mix(ul, sl, mul, 3 * RWKV_WIDTH)
    lw = _dot(jnp.tanh(xl).astype(bf16), wup[...])
    la = _dot(xl.astype(bf16), aup[...])
    lg = _dot(jax.nn.sigmoid(xl).astype(bf16), gup[...])
    z = -(w0[...] + lw)
    softplus = jnp.maximum(z, 0.0) + jnp.log(1.0 + jnp.exp(-jnp.abs(z)))
    w_log = -softplus - 0.5
    out_ref[3] = -jnp.exp(w_log)
    out_ref[4] = jax.nn.sigmoid(a0[...] + la)
    out_ref[5] = lg


def _rwkv_pre(proj, start_all, mu_all, w0, a0, wup, aup, gup, tt=256):
    n = proj.shape[1]
    npt = NP // tt
    wcols = 3 * RWKV_WIDTH + LORA_PAD
    lblk = 3 * RWKV_WIDTH // LORA_PAD
    u_spec = lambda g: pl.BlockSpec((1, tt, GROUP_W), lambda i, g=g: (g, i, 0))
    s_spec = lambda g: pl.BlockSpec((1, tt, GROUP_W), lambda i, g=g: (jnp.where(i >= npt, 1, 0), 0, g))
    m_spec = lambda g: pl.BlockSpec((1, GROUP_W), lambda i, g=g: (0, g))
    full = lambda shp: pl.BlockSpec(shp, lambda i: tuple(0 for _ in shp))
    return pl.pallas_call(
        functools.partial(_rwkv_pre_kernel, n_prompt_tiles=npt),
        grid=(n // tt,),
        in_specs=[u_spec(0), u_spec(1), u_spec(2),
                  pl.BlockSpec((1, tt, LORA_PAD), lambda i: (6, i, 0)),
                  s_spec(0), s_spec(1), s_spec(2),
                  pl.BlockSpec((1, tt, LORA_PAD), lambda i: (jnp.where(i >= npt, 1, 0), 0, lblk)),
                  m_spec(0), m_spec(1), m_spec(2),
                  pl.BlockSpec((1, LORA_PAD), lambda i: (0, lblk)),
                  full((1, RWKV_WIDTH)), full((1, RWKV_WIDTH)),
                  full((LORA_PAD, RWKV_WIDTH)), full((LORA_PAD, RWKV_WIDTH)), full((LORA_PAD, RWKV_WIDTH))],
        out_specs=pl.BlockSpec((6, tt, GROUP_W), lambda i: (0, i, 0)),
        out_shape=jax.ShapeDtypeStruct((6, n, GROUP_W), f32),
        scratch_shapes=[pltpu.VMEM((8, wcols), f32)],
        compiler_params=_cp(("arbitrary",)),
    )(proj, proj, proj, proj, start_all, start_all, start_all, start_all,
      mu_all, mu_all, mu_all, mu_all, w0, a0, wup, aup, gup)


def _split(a):
    hi = a.astype(bf16)
    return hi, (a - hi.astype(f32)).astype(bf16)


def _mm(a, b, passes, dims=(((1,), (0,)), ((), ()))):
    dot = lambda x, y: lax.dot_general(x, y, dims, preferred_element_type=f32)
    if passes == 1:
        return dot(a.astype(bf16), b.astype(bf16))
    a_hi, a_lo = _split(a)
    b_hi, b_lo = _split(b)
    return dot(a_hi, b_hi) + (dot(a_lo, b_hi) + dot(a_hi, b_lo))


_NT = (((1,), (1,)), ((), ()))
_TN = (((0,), (0,)), ((), ()))


def _prefix_sum_rows(x):
    rows = x.shape[0]
    row = lax.broadcasted_iota(i32, x.shape, 0)
    s = 1
    while s < rows:
        x = x + jnp.where(row >= s, pltpu.roll(x, s, 0), 0.0)
        s *= 2
    return x


def _chunk_prepare(chains):
    c, n = chains[0][0].shape
    row = lax.broadcasted_iota(i32, (c, c), 0)
    col = lax.broadcasted_iota(i32, (c, c), 1)
    tri_incl = col <= row
    strict = col < row
    each = lambda fn, *lists: [fn(*xs) for xs in zip(*lists)]
    rs, ks, vs, kks, bs, lws = (list(x) for x in zip(*chains))
    cums = each(_prefix_sum_rows, lws)
    tots = [cum[c - 1:c, :] for cum in cums]
    p_invs = [jnp.exp(-cum) for cum in cums]
    decs = each(lambda tot, cum: jnp.exp(tot - cum), tots, cums)
    kk_ts = each(lambda kk, cum, lw: kk * jnp.exp(cum - lw), kks, cums, lws)
    r_ts = each(lambda r, cum: r * jnp.exp(cum), rs, cums)
    gs = each(lambda kk_t, r_t, k, b, p_inv: _mm(jnp.concatenate([kk_t, r_t], axis=0),
                                                 jnp.concatenate([k * p_inv, b * p_inv], axis=0), 1, _NT),
              kk_ts, r_ts, ks, bs, p_invs)
    l_bs = [jnp.where(strict, g[:c, c:], 0.0) for g in gs]
    t_invs = [(row == col).astype(f32) for _ in chains]
    s = 1
    while s < c:
        sh = s.bit_length() - 1
        off = (((row >> sh) & 1) == 1) & ((col >> sh) == (row >> sh) - 1)
        tmps = each(lambda l_b, t: _mm(jnp.where(off, l_b, 0.0), t, 1), l_bs, t_invs)
        t_invs = each(lambda t, tmp: t - _mm(t, tmp, 1), t_invs, tmps)
        s *= 2
    lvs = each(lambda g, v: _mm(jnp.concatenate([jnp.where(strict, g[:c, :c], 0.0),
                                                 jnp.where(tri_incl, g[c:, :c], 0.0)], axis=0), v, 1), gs, vs)
    ws = each(lambda t, kk_t, lv: _mm(t, jnp.concatenate([kk_t, lv[:c]], axis=1), 1), t_invs, kk_ts, lvs)
    aws = each(lambda g, w: _mm(jnp.where(tri_incl, g[c:, c:], 0.0), w, 1), gs, ws)
    xws = each(lambda b, dec, w: _mm(b * dec, w, 1, _TN), bs, decs, ws)
    kvs = each(lambda k, dec, v: _mm(k * dec, v, 1, _TN), ks, decs, vs)
    eye = lax.broadcasted_iota(i32, (n, n), 0) == lax.broadcasted_iota(i32, (n, n), 1)
    zs = each(lambda r_t, aw, tot, xw: jnp.concatenate(
        [r_t - aw[:, :n], jnp.where(eye, jnp.broadcast_to(jnp.exp(tot), (n, n)), 0.0) - xw[:, :n]], axis=0),
        r_ts, aws, tots, xws)
    ys = each(lambda lv, aw, kv, xw: jnp.concatenate([lv[c:] - aw[:, n:], kv - xw[:, n:]], axis=0),
              lvs, aws, kvs, xws)
    return zs, ys


def _scan_prep_kernel(in_ref, kk_ref, ka_ref, rk_ref, z_ref, y_ref, bonus_ref, *, tv, heads, cpp):
    n = HEAD_DIM

    def padded(x):
        if tv < SCAN_CHUNK:
            return jnp.concatenate([x, jnp.zeros((SCAN_CHUNK - tv, x.shape[1]), f32)], axis=0)
        return x

    chains = []
    for ci in range(cpp):
        rows = slice(ci * tv, (ci + 1) * tv)
        xr2, xk2, xv2, lw2, a2 = (padded(in_ref[g, rows, :]) for g in range(5))
        bonus = []
        for h in range(heads):
            sl = slice(h * n, (h + 1) * n)
            xr, xk, xv, lw, a = xr2[:, sl], xk2[:, sl], xv2[:, sl], lw2[:, sl], a2[:, sl]
            kkr = xk * kk_ref[:, sl]
            nrm = jnp.sqrt(jnp.sum(kkr * kkr, axis=-1, keepdims=True))
            kk = kkr / jnp.maximum(nrm, 1e-12)
            km = xk * (1.0 + (a - 1.0) * ka_ref[:, sl])
            chains.append((xr, km, xv, kk, kk * a, lw))
            bonus.append(jnp.sum(xr[:tv] * km[:tv] * rk_ref[:, sl], axis=-1, keepdims=True) * xv[:tv])
        bonus_ref[rows, :] = jnp.concatenate(bonus, axis=1)
    zs, ys = _chunk_prepare(chains)
    for ci in range(cpp):
        for p in range(heads // 2):
            i0 = ci * heads + 2 * p
            z_ref[ci, p] = jnp.concatenate(zs[i0:i0 + 2], axis=1)
            y_ref[ci, p] = jnp.concatenate(ys[i0:i0 + 2], axis=1)


def _scan_apply_kernel(z_ref, y_ref, bonus_ref, g_ref, lnw_ref, lnb_ref, s0_ref, o_ref, sout_ref, m_scr, *, tv, cps):
    c = pl.program_id(2)
    n = HEAD_DIM

    @pl.when(c == 0)
    def _():
        m_scr[...] = s0_ref[0]

    m = [m_scr[0], m_scr[1]]
    for ci in range(cps):
        z2, y2 = z_ref[ci, 0], y_ref[ci, 0]
        rows = slice(ci * tv, (ci + 1) * tv)
        outs = []
        for h in range(2):
            sl = slice(h * n, (h + 1) * n)
            res = _mm(z2[:, sl], m[h], 3) + y2[:, sl]
            m[h] = res[SCAN_CHUNK:]
            o = res[:tv]
            mean = jnp.mean(o, axis=-1, keepdims=True)
            var = jnp.mean(jnp.square(o - mean), axis=-1, keepdims=True)
            outs.append((o - mean) * lax.rsqrt(var + GN_EPS) * lnw_ref[:, sl] + lnb_ref[:, sl])
        o_ref[rows, :] = (jnp.concatenate(outs, axis=1) + bonus_ref[rows, :]) * g_ref[0, rows, :]
    m_scr[0] = m[0]
    m_scr[1] = m[1]

    @pl.when(c == pl.num_programs(2) - 1)
    def _():
        sout_ref[0] = m_scr[...]


def _scan(scan_in, k_k, k_a, r_k, lnx_w, lnx_b, state0_t, *, n_seq, seq_len, row0):
    tv = min(seq_len, SCAN_CHUNK)
    n_chunks = seq_len // tv
    cpp = min(SCAN_PREP_CHUNKS, n_chunks)
    heads = min(SCAN_PREP_CHAINS // cpp, N_HEADS)
    psteps = n_chunks // cpp
    prow = cpp * tv
    lanes = heads * HEAD_DIM
    n_rows = n_seq * seq_len
    pair = (2 * SCAN_CHUNK, 2 * HEAD_DIM)
    par4 = lambda: pl.BlockSpec((1, lanes), lambda s, p, c: (0, p))
    zy_shape = jax.ShapeDtypeStruct((n_seq * n_chunks, N_HEADS // 2) + pair, f32)
    zy_spec = pl.BlockSpec((cpp, heads // 2) + pair, lambda s, p, c: (s * psteps + c, p, 0, 0))
    z, y, bonus = pl.pallas_call(
        functools.partial(_scan_prep_kernel, tv=tv, heads=heads, cpp=cpp),
        grid=(n_seq, N_HEADS // heads, psteps),
        in_specs=[pl.BlockSpec((5, prow, lanes), lambda s, p, c: (0, row0 // prow + s * psteps + c, p)),
                  par4(), par4(), par4()],
        out_specs=[zy_spec, zy_spec, pl.BlockSpec((prow, lanes), lambda s, p, c: (s * psteps + c, p))],
        out_shape=[zy_shape, zy_shape, jax.ShapeDtypeStruct((n_rows, RWKV_WIDTH), f32)],
        compiler_params=_cp(("arbitrary", "arbitrary", "arbitrary")),
    )(scan_in, k_k, k_a, r_k)

    cps = min(SCAN_APPLY_CHUNKS, n_chunks)
    steps = n_chunks // cps
    rows = cps * tv
    par2 = lambda: pl.BlockSpec((1, 2 * HEAD_DIM), lambda s, p, c: (0, p))
    zy2 = pl.BlockSpec((cps, 1) + pair, lambda s, p, c: (s * steps + c, p, 0, 0))
    row_spec = pl.BlockSpec((rows, 2 * HEAD_DIM), lambda s, p, c: (s * steps + c, p))
    state_spec = pl.BlockSpec((1, 2, HEAD_DIM, HEAD_DIM), lambda s, p, c: (s, p, 0, 0))
    return pl.pallas_call(
        functools.partial(_scan_apply_kernel, tv=tv, cps=cps),
        grid=(n_seq, N_HEADS // 2, steps),
        in_specs=[zy2, zy2, row_spec,
                  pl.BlockSpec((1, rows, 2 * HEAD_DIM), lambda s, p, c: (5, row0 // rows + s * steps + c, p)),
                  par2(), par2(), state_spec],
        out_specs=[row_spec, state_spec],
        out_shape=[jax.ShapeDtypeStruct((n_rows, RWKV_WIDTH), f32),
                   jax.ShapeDtypeStruct((n_seq, N_HEADS, HEAD_DIM, HEAD_DIM), f32)],
        scratch_shapes=[pltpu.VMEM((2, HEAD_DIM, HEAD_DIM), f32)],
        compiler_params=_cp(("arbitrary", "arbitrary", "arbitrary")),
    )(z, y, bonus, scan_in, lnx_w, lnx_b, state0_t)


def _kv_layout_kernel(k_ref, v_ref, km_ref, kbf_ref, vtbf_ref, kt_ref, vt_ref):
    k = k_ref[0]
    v_t = v_ref[0].T
    km_ref[0] = jnp.mean(k, axis=0, keepdims=True)
    kbf_ref[...] = k.astype(bf16)
    vtbf_ref[0] = v_t.astype(bf16)
    kt_ref[0] = k.T
    vt_ref[0] = v_t


def _kv_layout(proj, k_group=4, v_group=5):
    n_blocks = NP // MOBA_BLOCK
    per_seq = SEQ // MOBA_BLOCK
    blk = lambda g: pl.BlockSpec((1, MOBA_BLOCK, GROUP_W), lambda i, g=g: (g, i, 0))
    t_spec = pl.BlockSpec((1, GROUP_W, MOBA_BLOCK), lambda i: (i // per_seq, 0, i % per_seq))
    t_shape = jax.ShapeDtypeStruct((BATCH, GROUP_W, SEQ), f32)
    return pl.pallas_call(
        _kv_layout_kernel,
        grid=(n_blocks,),
        in_specs=[blk(k_group), blk(v_group)],
        out_specs=[pl.BlockSpec((1, 1, GROUP_W), lambda i: (i, 0, 0)),
                   pl.BlockSpec((MOBA_BLOCK, GROUP_W), lambda i: (i, 0)),
                   pl.BlockSpec((1, GROUP_W, MOBA_BLOCK), lambda i: (i, 0, 0)),
                   t_spec, t_spec],
        out_shape=[jax.ShapeDtypeStruct((n_blocks, 1, GROUP_W), f32),
                   jax.ShapeDtypeStruct((NP, GROUP_W), bf16),
                   jax.ShapeDtypeStruct((n_blocks, GROUP_W, MOBA_BLOCK), bf16),
                   t_shape, t_shape],
        compiler_params=_cp(("arbitrary",)),
    )(proj, proj)


def _bias_tile(rb_ref, head, d):
    bucket = _t5_bucket(d)
    bias = jnp.zeros(d.shape, f32)
    for kb in range(N_BUCKETS):
        bias = jnp.where(bucket == kb, rb_ref[kb, head], bias)
    return jnp.where(d >= 0, bias - rb_ref[N_BUCKETS - 1, head], NEG)


def _fold8(x, op):
    parts = [x[i * 8:(i + 1) * 8] for i in range(x.shape[0] // 8)]
    while len(parts) > 1:
        parts = [op(parts[i], parts[i + 1]) for i in range(0, len(parts), 2)]
    return parts[0]


def _attn_prompt_kernel(rb_ref, q_ref, k_ref, vt_ref, km_ref, o_ref, bias_scr, s_scr, sel_scr):
    p = pl.program_id(1)
    own = pl.program_id(2)
    nq = MOBA_BLOCK
    n_blocks = km_ref.shape[1]
    unroll = ATTN_UNROLL
    own_slot = s_scr.shape[0] - 1
    prev_slot = own_slot - 1
    n_far = jnp.maximum(own - 1, 0)
    n_groups = (n_far + unroll - 1) // unroll

    @pl.when(own == 0)
    def _():
        di = lax.broadcasted_iota(i32, (MOBA_BLOCK, nq), 1) - lax.broadcasted_iota(i32, (MOBA_BLOCK, nq), 0)
        for variant in range(2):
            bias_scr[variant] = jnp.concatenate(
                [_bias_tile(rb_ref, 2 * p + h, di + variant * MOBA_BLOCK) for h in range(2)], axis=1)

    q_t = q_ref[0].T
    row_head = lax.broadcasted_iota(i32, q_t.shape, 0) >> 6
    scale = HEAD_DIM ** -0.5
    prev = jnp.maximum(own - 1, 0)
    q2 = jnp.concatenate([jnp.where(row_head == h, q_t, 0.0) for h in range(2)], axis=1)
    sel_scr[...] = _topk_mask(_dot(km_ref[0], q2, HI), own, MOBA_TOPK, axis=0)
    qs = (q2 * scale).astype(bf16)

    def keys(j):
        return k_ref[pl.ds(pl.multiple_of(j * MOBA_BLOCK, MOBA_BLOCK), MOBA_BLOCK), :]

    def chosen(j, limit):
        return jnp.where(j < limit, sel_scr[pl.ds(jnp.minimum(j, n_blocks - 1), 1), :], 0.0) > 0.5

    s_own = _dot(keys(own), qs) + bias_scr[0]
    s_prev = _dot(keys(prev), qs) + bias_scr[1]
    s_scr[own_slot] = s_own
    s_scr[prev_slot] = s_prev
    mrun = jnp.maximum(_fold8(s_own, jnp.maximum),
                       jnp.where(chosen(prev, own), _fold8(s_prev, jnp.maximum), NEG))

    def pass1(g, mrun):
        for u in range(unroll):
            j = g * unroll + u
            s = _dot(keys(jnp.minimum(j, n_blocks - 1)), qs)
            s_scr[j] = s
            mrun = jnp.maximum(mrun, jnp.where(chosen(j, n_far), _fold8(s, jnp.maximum), NEG))
        return mrun

    mrun = lax.fori_loop(0, n_groups, pass1, mrun)
    m = jnp.max(mrun, axis=0, keepdims=True)

    def weighted(slot, j, keep):
        pj = jnp.exp(s_scr[slot] - jnp.where(keep, m, -NEG))
        return _fold8(pj, jnp.add), _dot(vt_ref[j], pj.astype(bf16))

    l_own, a_own = weighted(own_slot, own, True)
    l_prev, a_prev = weighted(prev_slot, prev, chosen(prev, own))

    def pass2(g, carry):
        lrun, acc = carry
        for u in range(unroll):
            j = g * unroll + u
            lj, aj = weighted(j, jnp.minimum(j, n_blocks - 1), chosen(j, n_far))
            lrun = lrun + lj
            acc = acc + aj
        return lrun, acc

    lrun, acc = lax.fori_loop(0, n_groups, pass2, (l_own + l_prev, a_own + a_prev))
    out2 = acc / jnp.sum(lrun, axis=0, keepdims=True)
    o_ref[...] = jnp.where(row_head == 0, out2[:, :nq], out2[:, nq:]).T


def _attn_prompt(rel_bias, proj, k_bf, vt_bf, kmean, q_group=3):
    n_blocks = SEQ // MOBA_BLOCK
    lanes = 2 * HEAD_DIM
    cols = 2 * MOBA_BLOCK
    return pl.pallas_call(
        _attn_prompt_kernel,
        grid=(BATCH, N_HEADS // 2, n_blocks),
        in_specs=[pl.BlockSpec(memory_space=pltpu.SMEM),
                  pl.BlockSpec((1, MOBA_BLOCK, lanes), lambda b, p, qi: (q_group, b * n_blocks + qi, p)),
                  pl.BlockSpec((SEQ, lanes), lambda b, p, qi: (b, p)),
                  pl.BlockSpec((n_blocks, lanes, MOBA_BLOCK), lambda b, p, qi: (b, p, 0)),
                  pl.BlockSpec((1, n_blocks, lanes), lambda b, p, qi: (b, 0, p))],
        out_specs=pl.BlockSpec((MOBA_BLOCK, lanes), lambda b, p, qi: (b * n_blocks + qi, p)),
        out_shape=jax.ShapeDtypeStruct((NP, ATT_WIDTH), f32),
        scratch_shapes=[pltpu.VMEM((2, MOBA_BLOCK, cols), f32),
                        pltpu.VMEM((n_blocks + ATTN_UNROLL + 2, MOBA_BLOCK, cols), f32),
                        pltpu.VMEM((n_blocks, cols), f32)],
        compiler_params=_cp(("arbitrary", "arbitrary", "arbitrary")),
    )(rel_bias, proj, k_bf, vt_bf, kmean)


def _attn_sample_kernel(pt_ref, rb_ref, q_ref, kn_ref, vn_ref, *rest):
    pps = PAGES_PER_STEP
    kp, vp = rest[:pps], rest[pps:2 * pps]
    o_ref = rest[2 * pps]
    s_scr, gate_scr, qm_scr, qc_scr, acc_scr, l_scr = rest[2 * pps + 1:]
    step = pl.program_id(1)
    hd = HEAD_DIM
    head_rows = lambda h: slice(h * DEC_SEQ, (h + 1) * DEC_SEQ)
    n_steps = N_PAGES // pps
    pages_per_block = MOBA_BLOCK // PAGE_SIZE
    rows = N_HEADS * DEC_SEQ
    row = lax.broadcasted_iota(i32, (rows, 1), 0)
    trow = row & (DEC_SEQ - 1)
    hrow = row >> 3
    blk = lax.broadcasted_iota(i32, (rows, N_PAST_BLOCKS), 1)

    @pl.when(step == 0)
    def _():
        qt = jnp.concatenate([q_ref[...]] * N_HEADS, axis=0)
        lane_head = lax.broadcasted_iota(i32, qt.shape, 1) >> 6
        qm_scr[...] = jnp.where(lane_head == hrow, qt * (HEAD_DIM ** -0.5), 0.0).astype(bf16)
        q = q_ref[...] * (HEAD_DIM ** -0.5)
        for h in range(N_HEADS):
            qc_scr[head_rows(h), :] = q[:, h * hd:(h + 1) * hd]
        gate_scr[...] = jnp.zeros_like(gate_scr)

    @pl.when(step < n_steps)
    def _():
        g = gate_scr[...]
        qh = [qc_scr[head_rows(h), :].astype(bf16) for h in range(N_HEADS)]
        for i in range(pps):
            page = step * pps + i
            sc = jnp.concatenate([_dot(qh[h], kp[i][0, 0, h].astype(bf16)) for h in range(N_HEADS)],
                                 axis=0)
            s_scr[page] = sc
            g = g + jnp.where(blk == page // pages_per_block, jnp.sum(sc, axis=1, keepdims=True), 0.0)
        gate_scr[...] = g

    @pl.when(step == n_steps - 1)
    def _():
        sel = _topk_mask(gate_scr[...], N_PAST_BLOCKS, MOBA_TOPK)
        d_own = trow - lax.broadcasted_iota(i32, (rows, DEC_SEQ), 1)
        s_own = _dot_nt(qm_scr[...], kn_ref[...].astype(bf16))
        b_own = jnp.zeros(s_own.shape, f32)
        for kb in range(DEC_SEQ):
            b_own = jnp.where(d_own == kb, rb_ref[:, kb:kb + 1], b_own)
        s_own = jnp.where(d_own >= 0, s_own + b_own, NEG)
        far_bias = rb_ref[:, N_BUCKETS - 1:N_BUCKETS]
        def masked(page, bias):
            chosen = jnp.sum(jnp.where(blk == page // pages_per_block, sel, 0.0), axis=1, keepdims=True) > 0.5
            sj = jnp.where(chosen, s_scr[page] + bias, NEG)
            s_scr[page] = sj
            return jnp.max(sj, axis=1, keepdims=True)

        n_far = N_PAGES - pages_per_block
        m = jnp.max(s_own, axis=1, keepdims=True)
        m = lax.fori_loop(0, n_far, lambda j, m: jnp.maximum(m, masked(j, far_bias)), m)
        for page in range(n_far, N_PAGES):
            d_near = (PAST_LEN - page * PAGE_SIZE) + trow - lax.broadcasted_iota(i32, (rows, PAGE_SIZE), 1)
            bucket = _t5_bucket(d_near)
            b_near = jnp.zeros(d_near.shape, f32)
            for kb in range(N_BUCKETS):
                b_near = jnp.where(bucket == kb, rb_ref[:, kb:kb + 1], b_near)
            m = jnp.maximum(m, masked(page, b_near))

        def expo(j, l):
            pj = jnp.exp(s_scr[j] - m)
            s_scr[j] = pj
            return l + jnp.sum(pj, axis=1, keepdims=True)

        p_own = jnp.exp(s_own - m)
        l_scr[...] = lax.fori_loop(0, N_PAGES, expo, jnp.sum(p_own, axis=1, keepdims=True))
        acc = jnp.zeros((rows, ATT_WIDTH), f32)
        vn = vn_ref[...]
        for t in range(DEC_SEQ):
            acc = acc + p_own[:, t:t + 1] * vn[t:t + 1, :]
        for h in range(N_HEADS):
            acc_scr[head_rows(h), :] = acc[head_rows(h), h * hd:(h + 1) * hd]

    @pl.when(step >= n_steps)
    def _():
        acc = acc_scr[...]
        for i in range(pps):
            page = (step - n_steps) * pps + i
            acc = acc + jnp.concatenate([_dot_nt(s_scr[page, head_rows(h), :].astype(bf16),
                                                 vp[i][0, 0, h].astype(bf16))
                                         for h in range(N_HEADS)], axis=0)
        acc_scr[...] = acc

    @pl.when(step == 2 * n_steps - 1)
    def _():
        a = acc_scr[...] / l_scr[...]
        o_ref[...] = jnp.concatenate([a[head_rows(h), :] for h in range(N_HEADS)], axis=1)


def _attn_sample(page_table, rb_rows, q_s, k_new, v_new, pool_k, pool_v):
    pps = PAGES_PER_STEP
    n_steps = N_PAGES // pps
    rows = N_HEADS * DEC_SEQ
    tok = pl.BlockSpec((DEC_SEQ, ATT_WIDTH), lambda b, s, pt: (b, 0))
    page_block = (1, 1, N_HEADS, HEAD_DIM, PAGE_SIZE)

    def kspec(i):
        return pl.BlockSpec(page_block, lambda b, s, pt, i=i:
                            (0, pt[b * N_PAGES + jnp.minimum(s, n_steps - 1) * pps + i], 0, 0, 0))

    def vspec(i):
        return pl.BlockSpec(page_block, lambda b, s, pt, i=i:
                            (0, pt[b * N_PAGES + jnp.maximum(s - n_steps, 0) * pps + i], 0, 0, 0))

    grid_spec = pltpu.PrefetchScalarGridSpec(
        num_scalar_prefetch=1,
        grid=(DEC_BATCH, 2 * n_steps),
        in_specs=[pl.BlockSpec((rows, N_BUCKETS), lambda b, s, pt: (0, 0)), tok, tok, tok]
                 + [kspec(i) for i in range(pps)] + [vspec(i) for i in range(pps)],
        out_specs=pl.BlockSpec((DEC_SEQ, ATT_WIDTH), lambda b, s, pt: (b, 0)),
        scratch_shapes=[pltpu.VMEM((N_PAGES, rows, PAGE_SIZE), f32),
                        pltpu.VMEM((rows, N_PAST_BLOCKS), f32),
                        pltpu.VMEM((rows, ATT_WIDTH), bf16),
                        pltpu.VMEM((rows, HEAD_DIM), f32),
                        pltpu.VMEM((rows, HEAD_DIM), f32),
                        pltpu.VMEM((rows, 1), f32)])
    return pl.pallas_call(
        _attn_sample_kernel,
        grid_spec=grid_spec,
        out_shape=jax.ShapeDtypeStruct((NS, ATT_WIDTH), f32),
        compiler_params=_cp(("arbitrary", "arbitrary")),
    )(page_table.reshape(-1), rb_rows, q_s, k_new, v_new, *([pool_k] * pps), *([pool_v] * pps))


def _outproj_kernel(x_ref, ro_ref, ao_ref, wo_ref, g2_ref, rw_ref, rb_ref, h_ref, xn_ref, gate_ref, mask_ref):
    h = (x_ref[...] + _dot(ro_ref[...].astype(bf16), wo_ref[:RWKV_WIDTH, :])
         + _dot(ao_ref[...].astype(bf16), wo_ref[RWKV_WIDTH:, :]))
    h_ref[...] = h
    xn = _rms(h, g2_ref[...])
    xn_ref[...] = xn.astype(bf16)
    logits = _dot(xn, rw_ref[...], HI) + rb_ref[...]
    sel = _topk_mask(logits, N_EXPERTS, TOP_K)
    mx = jnp.max(jnp.where(sel > 0.5, logits, NEG), axis=1, keepdims=True)
    e = jnp.where(sel > 0.5, jnp.exp(logits - mx), 0.0)
    gate_ref[...] = e / jnp.sum(e, axis=1, keepdims=True)
    mask_ref[...] = sel


def _outproj(x_all, ro, ao, wo_bf, g2, router_w, router_b, tm=256):
    n, d = x_all.shape
    row = lambda w: pl.BlockSpec((tm, w), lambda i: (i, 0))
    full = lambda shp: pl.BlockSpec(shp, lambda i: tuple(0 for _ in shp))
    return pl.pallas_call(
        _outproj_kernel,
        grid=(n // tm,),
        in_specs=[row(d), row(RWKV_WIDTH), row(ATT_WIDTH), full((d, d)), full((1, d)),
                  full((d, N_EXPERTS)), full((1, N_EXPERTS))],
        out_specs=[row(d), row(d), row(N_EXPERTS), row(N_EXPERTS)],
        out_shape=[jax.ShapeDtypeStruct((n, d), f32), jax.ShapeDtypeStruct((n, d), bf16),
                   jax.ShapeDtypeStruct((n, N_EXPERTS), f32), jax.ShapeDtypeStruct((n, N_EXPERTS), f32)],
        compiler_params=_cp(("arbitrary",)),
    )(x_all, ro, ao, wo_bf, g2, router_w, router_b)


def _expert_changed(te_ref, t):
    return jnp.logical_or(t == 0, te_ref[t] != te_ref[jnp.maximum(t - 1, 0)])


def _moe_up_kernel(te_ref, tv_ref, x_ref, wg_ref, wl_ref, bg_ref, bl_ref, act_ref, wg_s, wl_s):
    t = pl.program_id(1)

    @pl.when(_expert_changed(te_ref, t))
    def _():
        wg_s[...] = wg_ref[0].astype(bf16)
        wl_s[...] = wl_ref[0].astype(bf16)

    @pl.when(tv_ref[t] == 1)
    def _():
        x = x_ref[...]
        glu = jnp.minimum(_dot(x, wg_s[...]) + bg_ref[0], SWIGLU_LIMIT)
        lin = jnp.clip(_dot(x, wl_s[...]) + bl_ref[0], -SWIGLU_LIMIT, SWIGLU_LIMIT)
        act_ref[...] = (glu * jax.nn.sigmoid(SWIGLU_ALPHA * glu) * (lin + 1.0)).astype(bf16)

    @pl.when(tv_ref[t] == 0)
    def _():
        act_ref[...] = jnp.zeros_like(act_ref)


def _moe_up(tile_expert, tile_valid, x_sorted, w_gu, b_gu3, tf=512):
    nj = D_FF // tf
    grid_spec = pltpu.PrefetchScalarGridSpec(
        num_scalar_prefetch=2,
        grid=(nj, MOE_TILES),
        in_specs=[pl.BlockSpec((MOE_TM, D_MODEL), lambda j, t, te, tv: (t, 0)),
                  pl.BlockSpec((1, D_MODEL, tf), lambda j, t, te, tv: (te[t], 0, j)),
                  pl.BlockSpec((1, D_MODEL, tf), lambda j, t, te, tv: (te[t], 0, nj + j)),
                  pl.BlockSpec((1, 1, tf), lambda j, t, te, tv: (te[t], 0, j)),
                  pl.BlockSpec((1, 1, tf), lambda j, t, te, tv: (te[t], 0, nj + j))],
        out_specs=pl.BlockSpec((MOE_TM, tf), lambda j, t, te, tv: (t, j)),
        scratch_shapes=[pltpu.VMEM((D_MODEL, tf), bf16), pltpu.VMEM((D_MODEL, tf), bf16)])
    return pl.pallas_call(
        _moe_up_kernel,
        grid_spec=grid_spec,
        out_shape=jax.ShapeDtypeStruct((MOE_ROWS, D_FF), bf16),
        compiler_params=_cp(("arbitrary", "arbitrary")),
    )(tile_expert, tile_valid, x_sorted, w_gu, w_gu, b_gu3, b_gu3)


def _moe_dn_kernel(te_ref, tv_ref, act_ref, wd_ref, bd_ref, gs_ref, y_ref, wd_s):
    t = pl.program_id(1)

    @pl.when(_expert_changed(te_ref, t))
    def _():
        wd_s[...] = wd_ref[0].astype(bf16)

    @pl.when(tv_ref[t] == 1)
    def _():
        y_ref[...] = ((_dot(act_ref[...], wd_s[...]) + bd_ref[0]) * gs_ref[...]).astype(y_ref.dtype)

    @pl.when(tv_ref[t] == 0)
    def _():
        y_ref[...] = jnp.zeros_like(y_ref)


def _moe_dn(tile_expert, tile_valid, act, w_dn, b_dn3, gate_sorted, tn=1024):
    nj = D_MODEL // tn
    grid_spec = pltpu.PrefetchScalarGridSpec(
        num_scalar_prefetch=2,
        grid=(nj, MOE_TILES),
        in_specs=[pl.BlockSpec((MOE_TM, D_FF), lambda j, t, te, tv: (t, 0)),
                  pl.BlockSpec((1, D_FF, tn), lambda j, t, te, tv: (te[t], 0, j)),
                  pl.BlockSpec((1, 1, tn), lambda j, t, te, tv: (te[t], 0, j)),
                  pl.BlockSpec((MOE_TM, 1), lambda j, t, te, tv: (t, 0))],
        out_specs=pl.BlockSpec((MOE_TM, tn), lambda j, t, te, tv: (t, j)),
        scratch_shapes=[pltpu.VMEM((D_FF, tn), bf16)])
    return pl.pallas_call(
        _moe_dn_kernel,
        grid_spec=grid_spec,
        out_shape=jax.ShapeDtypeStruct((MOE_ROWS, D_MODEL), bf16),
        compiler_params=_cp(("arbitrary", "arbitrary")),
    )(tile_expert, tile_valid, act, w_dn, b_dn3, gate_sorted)


def _route(gates, mask):
    n = gates.shape[0]
    mi = mask.astype(i32)
    counts = jnp.sum(mi, axis=0)
    rank = jnp.cumsum(mi, axis=0) - mi
    tiles_per = (counts + MOE_TM - 1) // MOE_TM
    tiles_end = jnp.cumsum(tiles_per)
    row_start = (tiles_end - tiles_per) * MOE_TM
    dest = row_start[None, :] + rank
    used = tiles_end[-1]
    t = jnp.arange(MOE_TILES, dtype=i32)
    te = jnp.sum((tiles_end[None, :] <= t[:, None]).astype(i32), axis=1)
    last_e = jnp.max(jnp.where(tiles_per > 0, jnp.arange(N_EXPERTS, dtype=i32), 0))
    tile_valid = (t < used).astype(i32)
    tile_expert = jnp.where(t < used, jnp.minimum(te, N_EXPERTS - 1), last_e)
    _, idx4 = lax.top_k(mask, TOP_K)
    in_bounds = dict(mode="promise_in_bounds")
    dest4 = jnp.take_along_axis(dest, idx4, axis=1, **in_bounds)
    g4 = jnp.take_along_axis(gates, idx4, axis=1, **in_bounds)
    order = jnp.argsort(idx4.reshape(-1), stable=True).astype(i32)
    per_row = lambda per_expert: jnp.repeat(per_expert[tile_expert], MOE_TM)
    off = jnp.arange(MOE_ROWS, dtype=i32) - per_row(row_start)
    live = (off < per_row(counts)) & (jnp.repeat(tile_valid, MOE_TM) == 1)
    src = order[jnp.clip(off + per_row(jnp.cumsum(counts) - counts), 0, n * TOP_K - 1)]
    sorted_tok = jnp.where(live, src // TOP_K, 0)
    gate_sorted = jnp.where(live, g4.reshape(-1)[src], 0.0)
    return tile_expert, tile_valid, sorted_tok, gate_sorted.reshape(-1, 1), dest4


def _final_kernel(h_ref, y4_ref, pe_ref, gn_ref, gw_ref, pp_ref, fn_ref, op_ref, os_ref, *, n_prompt_tiles):
    h = h_ref[...]
    y = y4_ref[0].astype(f32)
    for kk in range(1, TOP_K):
        y = y + y4_ref[kk].astype(f32)
    h = h + y
    gate = jax.nn.sigmoid(_dot(_rms(h, gn_ref[...]).astype(bf16), gw_ref[...]))
    h = h + gate * _dot(pe_ref[...].astype(bf16), pp_ref[...])
    out = _rms(h, fn_ref[...])
    i = pl.program_id(0)

    @pl.when(i < n_prompt_tiles)
    def _():
        op_ref[...] = out

    @pl.when(i >= n_prompt_tiles)
    def _():
        os_ref[...] = out


def _final(h1, y4, pe, ple_norm_g, gw_bf, pp_bf, final_norm_g, tm=128):
    n, d = h1.shape
    npt = NP // tm
    full = lambda shp: pl.BlockSpec(shp, lambda i: tuple(0 for _ in shp))
    return pl.pallas_call(
        functools.partial(_final_kernel, n_prompt_tiles=npt),
        grid=(n // tm,),
        in_specs=[pl.BlockSpec((tm, d), lambda i: (i, 0)),
                  pl.BlockSpec((TOP_K, tm, d), lambda i: (0, i, 0)),
                  pl.BlockSpec((tm, D_PLE), lambda i: (i, 0)),
                  full((1, d)), full((d, d)), full((D_PLE, d)), full((1, d))],
        out_specs=[pl.BlockSpec((tm, d), lambda i: (jnp.minimum(i, npt - 1), 0)),
                   pl.BlockSpec((tm, d), lambda i: (jnp.maximum(i - npt, 0), 0))],
        out_shape=[jax.ShapeDtypeStruct((NP, d), f32), jax.ShapeDtypeStruct((NS, d), f32)],
        compiler_params=_cp(("arbitrary",)),
    )(h1, y4, pe, ple_norm_g, gw_bf, pp_bf, final_norm_g)


def kernel(x_prompt, x_sample, cache_k, cache_v, state_wkv, state_shift, page_table, p_prompt, p_sample, norm1_g, w_in, mu_shift, w0, w_up, a0, a_up, g_up, k_k, k_a, r_k, lnx_w, lnx_b, w_out, rel_bias, norm2_g, router_w, router_b, w_gu, b_gu, w_dn, b_dn, ple_norm_g, ple_gate_w, ple_proj, final_norm_g):
    rw = RWKV_WIDTH
    x_all = jnp.concatenate([x_prompt.reshape(NP, D_MODEL), x_sample.reshape(NS, D_MODEL)], axis=0)
    pe = jnp.concatenate([p_prompt[0].reshape(NP, D_PLE), p_sample[0].reshape(NS, D_PLE)], axis=0)

    w = w_in[0]
    w_big = jnp.concatenate([w[:, :3 * rw], w[:, N_SHIFT:],
                             jnp.pad(w[:, 3 * rw:N_SHIFT], ((0, 0), (0, GROUP_W - LORA_WIDTH)))], axis=1).astype(bf16)
    proj = _inproj(x_all, norm1_g, w_big)

    pad_l = lambda a: jnp.pad(a, ((0, 0), (0, LORA_PAD - LORA_WIDTH)))
    start_s = jnp.repeat(state_shift[0], DEC_SEQ, axis=0)
    start_all = jnp.stack([jnp.zeros((NS, 3 * rw + LORA_PAD), f32), pad_l(start_s)])
    mu_all = pad_l(mu_shift)
    rows_pad = lambda a, r0: jnp.pad(a, ((r0, LORA_PAD - r0 - a.shape[0]), (0, 0))).astype(bf16)
    scan_in = _rwkv_pre(proj, start_all, mu_all, w0, a0,
                        rows_pad(w_up[0], 0), rows_pad(a_up[0], DECAY_LORA), rows_pad(g_up[0], DECAY_LORA + ICLR_LORA))
    rk_flat = r_k.reshape(1, rw)
    zero_state = jnp.zeros((BATCH, N_HEADS, HEAD_DIM, HEAD_DIM), f32)
    ro_p, wkv_p = _scan(scan_in, k_k, k_a, rk_flat, lnx_w, lnx_b, zero_state, n_seq=BATCH, seq_len=SEQ, row0=0)
    ro_s, wkv_s = _scan(scan_in, k_k, k_a, rk_flat, lnx_w, lnx_b, jnp.swapaxes(state_wkv[0], -1, -2),
                        n_seq=DEC_BATCH, seq_len=DEC_SEQ, row0=NP)

    kmean, k_bf, vt_bf, k_t, v_t = _kv_layout(proj)
    ao_p = _attn_prompt(rel_bias, proj, k_bf, vt_bf, kmean.reshape(BATCH, SEQ // MOBA_BLOCK, ATT_WIDTH))
    rb_rows = jnp.repeat(rel_bias.T, DEC_SEQ, axis=0)
    pages_t = lambda c: jnp.transpose(c, (0, 1, 3, 4, 2))
    ao_s = _attn_sample(page_table, rb_rows, proj[3, NP:], proj[4, NP:], proj[5, NP:],
                        pages_t(cache_k), pages_t(cache_v))

    ro = jnp.concatenate([ro_p, ro_s], axis=0)
    ao = jnp.concatenate([ao_p, ao_s], axis=0)
    h1, xn2, gates, mask = _outproj(x_all, ro, ao, w_out[0].astype(bf16), norm2_g, router_w[0], router_b)

    tile_expert, tile_valid, sorted_tok, gate_sorted, dest4 = _route(gates, mask)
    x_sorted = xn2.at[sorted_tok].get(mode="promise_in_bounds")
    act = _moe_up(tile_expert, tile_valid, x_sorted, w_gu[0], b_gu[0].reshape(N_EXPERTS, 1, 2 * D_FF))
    y_sorted = _moe_dn(tile_expert, tile_valid, act, w_dn[0], b_dn[0].reshape(N_EXPERTS, 1, D_MODEL), gate_sorted)
    y4 = y_sorted.at[dest4.T].get(mode="promise_in_bounds")

    y_p, y_s = _final(h1, y4, pe, ple_norm_g, ple_gate_w[0].astype(bf16), ple_proj[0].astype(bf16),
                      final_norm_g.reshape(1, D_MODEL))

    heads = lambda a, b, t: a.reshape(1, b, t, N_HEADS, HEAD_DIM)
    time_major = lambda a: jnp.transpose(a.reshape(1, BATCH, N_HEADS, HEAD_DIM, SEQ), (0, 1, 4, 2, 3))
    def shift_rows(row0, n_seq, seq_len):
        last = lambda g, w: proj[g, row0 + seq_len - 1:row0 + n_seq * seq_len:seq_len, :w]
        return jnp.concatenate([last(0, rw), last(1, rw), last(2, rw), last(6, LORA_WIDTH)], axis=-1)

    last_p = (0, BATCH, SEQ)
    last_s = (NP, DEC_BATCH, DEC_SEQ)
    return (y_p.reshape(BATCH, SEQ, D_MODEL),
            y_s.reshape(DEC_BATCH, DEC_SEQ, D_MODEL),
            time_major(k_t),
            time_major(v_t),
            jnp.swapaxes(wkv_p, -1, -2)[None],
            shift_rows(*last_p)[None],
            heads(proj[4, NP:], DEC_BATCH, DEC_SEQ),
            heads(proj[5, NP:], DEC_BATCH, DEC_SEQ),
            jnp.swapaxes(wkv_s, -1, -2)[None],
            shift_rows(*last_s)[None])
```
